```python
import jax, jax.numpy as jnp
from jax import lax
import numpy as np

D_MODEL = 2048
BATCH = 4
SEQ = 2048
DEPTH = 2

CHUNK = 64
Q_BLOCK = 128
HEAD_DIM = 128
MIX_WIDTH = D_MODEL // 4
N_BRANCH = 4
N_HEADS_FOX = MIX_WIDTH // HEAD_DIM
N_HEADS_CHUNK = MIX_WIDTH // HEAD_DIM
FOX_WIDTH = N_HEADS_FOX * HEAD_DIM
CHUNK_WIDTH = N_HEADS_CHUNK * HEAD_DIM
POOL_WIDTH = MIX_WIDTH
POOL_WINDOWS = (2, 4, 8, 16)
POOL_GROUPS = len(POOL_WINDOWS)
POOL_GROUP = POOL_WIDTH // POOL_GROUPS
CONV_WIDTH = MIX_WIDTH
CONV_K = 3
LEFT_CHUNKS = 8
BAND = LEFT_CHUNKS + 1
REL_CLIP = 256
REL_SIZE = REL_CLIP + CHUNK
PEER_HEADS = 8
PEER_KEYS = 128
PEER_EXPERTS = PEER_KEYS * PEER_KEYS
PEER_QDIM = 256
PEER_HALF = PEER_QDIM // 2
PEER_TOPK = 16
PEER_TOKEN_BLOCK = 128
EPS = 1e-6

SPLIT_SIZES = (FOX_WIDTH, FOX_WIDTH, FOX_WIDTH, N_HEADS_FOX,
               POOL_WIDTH,
               CONV_WIDTH, CONV_WIDTH, CONV_WIDTH,
               CHUNK_WIDTH, CHUNK_WIDTH, CHUNK_WIDTH,
               N_BRANCH * D_MODEL)
SPLIT_POINTS = tuple(sum(SPLIT_SIZES[:i + 1]) for i in range(len(SPLIT_SIZES) - 1))
IN_WIDTH = sum(SPLIT_SIZES)
FORGET_OFFSET = 3 * FOX_WIDTH

kernel_name = "hybrid_chunk_causal_peer_block"


def rmsnorm(x, g):
    x32 = x.astype(jnp.float32)
    y = x32 * lax.rsqrt(jnp.mean(x32 * x32, axis=-1, keepdims=True) + EPS)
    return y.astype(x.dtype) * g


def forgetting_attention(q, k, v, log_f):
    b, s, h, dh = q.shape
    scale = dh ** -0.5
    cum = jnp.cumsum(log_f, axis=1).transpose(0, 2, 1)
    outs = []
    for i in range(s // Q_BLOCK):
        q0, q1 = i * Q_BLOCK, (i + 1) * Q_BLOCK
        logits = jnp.einsum('bqhd,bkhd->bhqk', q[:, q0:q1], k[:, :q1]).astype(jnp.float32) * scale
        decay = cum[:, :, q0:q1, None] - cum[:, :, None, :q1]
        causal = jnp.arange(q1)[None, :] <= jnp.arange(q0, q1)[:, None]
        logits = jnp.where(causal, logits + decay, -jnp.inf)
        p = jax.nn.softmax(logits, axis=-1)
        outs.append(jnp.einsum('bhqk,bkhd->bqhd', p.astype(v.dtype), v[:, :q1]))
    return jnp.concatenate(outs, axis=1)


def multiscale_pool(p, group_w, scale):
    b, s, c = p.shape
    pg = p.reshape(b, s, POOL_GROUPS, POOL_GROUP).astype(jnp.float32)
    csum = jnp.concatenate([jnp.zeros((b, 1, POOL_GROUPS, POOL_GROUP), jnp.float32),
                            jnp.cumsum(pg, axis=1)], axis=1)
    t = jnp.arange(s)
    outs = []
    for gi, w in enumerate(POOL_WINDOWS):
        start = jnp.maximum(t + 1 - w, 0)
        count = (t + 1 - start).astype(jnp.float32)
        cg = csum[:, :, gi]
        mean = (cg[:, 1:] - cg[:, start]) / count[None, :, None]
        outs.append(mean - pg[:, :, gi])
    pooled = jnp.stack(outs, axis=2).astype(p.dtype)
    mixed = jnp.einsum('bsgc,gcd->bsgd', pooled, group_w)
    return mixed.reshape(b, s, c) * scale


def short_gated_conv(hx, gate_b, gate_c, conv_w):
    s = hx.shape[1]
    z = gate_c * hx
    zp = jnp.pad(z, ((0, 0), (CONV_K - 1, 0), (0, 0)))
    acc = conv_w[0] * zp[:, 0:s]
    for j in range(1, CONV_K):
        acc = acc + conv_w[j] * zp[:, j:j + s]
    return gate_b * acc


def chunked_rel_attention(q, k, v, rel_bias):
    b, s, h, dh = q.shape
    nc = s // CHUNK
    qc = q.reshape(b, nc, CHUNK, h, dh)

    def band_of(x):
        xc = x.reshape(b, nc, CHUNK, h, dh)
        xp = jnp.pad(xc, ((0, 0), (LEFT_CHUNKS, 0), (0, 0), (0, 0), (0, 0)))
        return jnp.stack([xp[:, j:j + nc] for j in range(BAND)], axis=2).reshape(b, nc, BAND * CHUNK, h, dh)

    kb, vb = band_of(k), band_of(v)
    logits = jnp.einsum('bcqhd,bckhd->bchqk', qc, kb).astype(jnp.float32) * dh ** -0.5
    qpos = LEFT_CHUNKS * CHUNK + jnp.arange(CHUNK)
    kpos = jnp.arange(BAND * CHUNK)
    dist = qpos[:, None] - kpos[None, :]
    idx = jnp.clip(dist, -(CHUNK - 1), REL_CLIP) + (CHUNK - 1)
    bias = rel_bias[:, idx].astype(jnp.float32)
    chunk_id = jnp.arange(nc)[:, None] + (kpos // CHUNK)[None, :] - LEFT_CHUNKS
    valid = (chunk_id >= 0)[None, :, None, None, :]
    logits = jnp.where(valid, logits + bias, -jnp.inf)
    p = jax.nn.softmax(logits, axis=-1)
    out = jnp.einsum('bchqk,bckhd->bcqhd', p.astype(v.dtype), vb)
    return out.reshape(b, s, h, dh)


def peer_ffn(x, w_q, sub_keys, expert_u, expert_v):
    b, s, d = x.shape
    q = (x @ w_q).reshape(b, s, PEER_HEADS, 2, PEER_HALF)
    scores = jnp.einsum('bshpc,hpkc->bshpk', q, sub_keys).astype(jnp.float32)
    top_s, top_i = lax.top_k(scores, PEER_TOPK)
    cand_s = (top_s[..., 0, :, None] + top_s[..., 1, None, :]).reshape(b, s, PEER_HEADS, PEER_TOPK * PEER_TOPK)
    cand_i = (top_i[..., 0, :, None] * PEER_KEYS + top_i[..., 1, None, :]).reshape(b, s, PEER_HEADS, PEER_TOPK * PEER_TOPK)
    best_s, pos = lax.top_k(cand_s, PEER_TOPK)
    expert_idx = jnp.take_along_axis(cand_i, pos, axis=-1)
    gates = jax.nn.softmax(best_s, axis=-1).astype(x.dtype)
    n_blk = (b * s) // PEER_TOKEN_BLOCK
    xt = x.reshape(n_blk, PEER_TOKEN_BLOCK, d)
    it = expert_idx.reshape(n_blk, PEER_TOKEN_BLOCK, PEER_HEADS * PEER_TOPK)
    gt = gates.reshape(n_blk, PEER_TOKEN_BLOCK, PEER_HEADS * PEER_TOPK)

    def token_block(args):
        xb, ib, gb = args
        act = jax.nn.gelu(jnp.einsum('tkd,td->tk', expert_u[ib], xb))
        return jnp.einsum('tk,tkd->td', gb * act, expert_v[ib])

    y = lax.map(token_block, (xt, it, gt))
    return y.reshape(b, s, d)


def setup_inputs(seed: int = 0) -> dict:
    key = jax.random.key(seed)
    ks = jax.random.split(key, 17)
    f32 = jnp.float32

    def nrm(k, shape, scale):
        return jax.random.normal(k, shape, f32) * scale

    x = nrm(ks[0], (BATCH, SEQ, D_MODEL), 1.0)
    norm_mix = 1.0 + nrm(ks[1], (DEPTH, D_MODEL), 0.05)
    w_in = nrm(ks[2], (DEPTH, D_MODEL, IN_WIDTH), D_MODEL ** -0.5)
    b_in = nrm(ks[3], (DEPTH, IN_WIDTH), 0.01)
    forget_bias = jax.random.uniform(ks[4], (DEPTH, N_HEADS_FOX), f32, 1.0, 4.0)
    b_in = b_in.at[:, FORGET_OFFSET:FORGET_OFFSET + N_HEADS_FOX].set(forget_bias)
    pool_w = nrm(ks[5], (DEPTH, POOL_GROUPS, POOL_GROUP, POOL_GROUP), POOL_GROUP ** -0.5)
    pool_scale = 1.0 + nrm(ks[6], (DEPTH, POOL_WIDTH), 0.1)
    conv_w = nrm(ks[7], (DEPTH, CONV_K, CONV_WIDTH), CONV_K ** -0.5)
    rel_bias = nrm(ks[8], (DEPTH, N_HEADS_CHUNK, REL_SIZE), 0.5)
    w_branch = nrm(ks[9], (DEPTH, N_BRANCH, MIX_WIDTH, D_MODEL), MIX_WIDTH ** -0.5)
    w_out = nrm(ks[10], (DEPTH, D_MODEL, D_MODEL), D_MODEL ** -0.5)
    norm_ffn = 1.0 + nrm(ks[11], (DEPTH, D_MODEL), 0.05)
    peer_wq = nrm(ks[12], (DEPTH, D_MODEL, PEER_HEADS * PEER_QDIM), D_MODEL ** -0.5)
    peer_keys = nrm(ks[13], (DEPTH, PEER_HEADS, 2, PEER_KEYS, PEER_HALF), PEER_HALF ** -0.5)
    peer_u = nrm(ks[14], (DEPTH, PEER_EXPERTS, D_MODEL), D_MODEL ** -0.5)
    peer_v = nrm(ks[15], (DEPTH, PEER_EXPERTS, D_MODEL), PEER_HEADS ** -0.5)
    norm_final = 1.0 + nrm(ks[16], (D_MODEL,), 0.05)
    return {"x": x, "norm_mix": norm_mix, "w_in": w_in, "b_in": b_in, "pool_w": pool_w,
            "pool_scale": pool_scale, "conv_w": conv_w, "rel_bias": rel_bias, "w_branch": w_branch,
            "w_out": w_out, "norm_ffn": norm_ffn, "peer_wq": peer_wq, "peer_keys": peer_keys,
            "peer_u": peer_u, "peer_v": peer_v, "norm_final": norm_final}


def reference(x, norm_mix, w_in, b_in, pool_w, pool_scale, conv_w, rel_bias, w_branch,
              w_out, norm_ffn, peer_wq, peer_keys, peer_u, peer_v, norm_final):
    b, s, d = x.shape
    h = x
    for l in range(DEPTH):
        xn = rmsnorm(h, norm_mix[l])
        proj = xn @ w_in[l] + b_in[l]
        (fq, fk, fv, f_logit, pool_in, conv_h, conv_b, conv_c,
         cq, ck, cv, gate_logit) = jnp.split(proj, SPLIT_POINTS, axis=-1)

        heads = lambda t: t.reshape(b, s, -1, HEAD_DIM)
        log_f = jax.nn.log_sigmoid(f_logit.astype(jnp.float32))
        y_fox = forgetting_attention(heads(fq), heads(fk), heads(fv), log_f).reshape(b, s, FOX_WIDTH)
        y_pool = multiscale_pool(pool_in, pool_w[l], pool_scale[l])
        y_conv = short_gated_conv(conv_h, conv_b, conv_c, conv_w[l])
        y_chunk = chunked_rel_attention(heads(cq), heads(ck), heads(cv), rel_bias[l]).reshape(b, s, CHUNK_WIDTH)

        branches = jnp.stack([y_fox, y_pool, y_conv, y_chunk], axis=2)
        branch_d = jnp.einsum('bsnc,ncd->bsnd', branches, w_branch[l])
        gates = jax.nn.sigmoid(gate_logit.reshape(b, s, N_BRANCH, d))
        merged = jnp.sum(gates * branch_d, axis=2)
        h = h + merged @ w_out[l]

        hn = rmsnorm(h, norm_ffn[l])
        h = h + peer_ffn(hn, peer_wq[l], peer_keys[l], peer_u[l], peer_v[l])
    return rmsnorm(h, norm_final)
```

```python
import functools

import numpy as np
import jax
import jax.numpy as jnp
from jax import lax
from jax.experimental import pallas as pl
from jax.experimental.pallas import tpu as pltpu

F32 = jnp.float32
BF16 = jnp.bfloat16

D_MODEL = 2048
HEAD_DIM = 128
MIX_WIDTH = D_MODEL // 4
N_BRANCH = 4
N_HEADS = MIX_WIDTH // HEAD_DIM
POOL_WINDOWS = (2, 4, 8, 16)
CONV_K = 3
CHUNK = 64
LEFT_CHUNKS = 8
REL_CLIP = 256
PEER_HEADS = 8
PEER_KEYS = 128
PEER_EXPERTS = PEER_KEYS * PEER_KEYS
PEER_TOPK = 16
EPS = 1e-6
FORGET_OFFSET = 3 * MIX_WIDTH

LANES = 128
VMEM_LIMIT_BYTES = 56 * 2**20

NEG = -1e30

NORM_TM = 512
MM_TM, MM_TN = 1024, 512
ATT_TQ = 256
MERGE_TM, MERGE_TN = 512, 512
SEL_TS = 128
PEER_TM, PEER_EB = 512, 512


def _params(*sem):
    return pltpu.CompilerParams(dimension_semantics=sem, vmem_limit_bytes=VMEM_LIMIT_BYTES)


def _norm_kernel(*refs, has_add, emit_sum):
    if has_add:
        x_ref, y_ref, g_ref = refs[:3]
        x = x_ref[...] + y_ref[...]
    else:
        x_ref, g_ref = refs[:2]
        x = x_ref[...]
    outs = refs[3 if has_add else 2:]
    if emit_sum:
        outs[0][...] = x
    inv = lax.rsqrt(jnp.mean(x * x, axis=-1, keepdims=True) + EPS)
    outs[-1][...] = ((x * inv) * g_ref[...]).astype(outs[-1].dtype)


def rmsnorm(x, g, out_dtype, add=None, emit_sum=False):
    t, d = x.shape
    row = pl.BlockSpec((NORM_TM, d), lambda i: (i, 0))
    ins = [x] + ([add] if add is not None else []) + [g.reshape(1, d)]
    in_specs = [row] * (len(ins) - 1) + [pl.BlockSpec((1, d), lambda i: (0, 0))]
    out_shape = [jax.ShapeDtypeStruct((t, d), out_dtype)]
    if emit_sum:
        out_shape = [jax.ShapeDtypeStruct((t, d), F32)] + out_shape
    res = pl.pallas_call(
        functools.partial(_norm_kernel, has_add=add is not None, emit_sum=emit_sum),
        grid=(t // NORM_TM,), in_specs=in_specs, out_specs=[row] * len(out_shape),
        out_shape=out_shape, compiler_params=_params("parallel"), name="rmsnorm")(*ins)
    return res if emit_sum else res[0]


def _matmul_kernel(*refs, has_res):
    a_ref, b_ref, bias_ref = refs[:3]
    o_ref = refs[-1]
    acc = jnp.dot(a_ref[...], b_ref[...], preferred_element_type=F32) + bias_ref[...]
    if has_res:
        acc = acc + refs[3][...]
    o_ref[...] = acc.astype(o_ref.dtype)


def matmul(a, b, bias, out_dtype, res=None, name="matmul"):
    m, k = a.shape
    n = b.shape[1]
    tm, tn = min(MM_TM, m), min(MM_TN, n)
    ins = [a, b, bias.reshape(1, n)] + ([res] if res is not None else [])
    in_specs = [pl.BlockSpec((tm, k), lambda i, j: (i, 0)),
                pl.BlockSpec((k, tn), lambda i, j: (0, j)),
                pl.BlockSpec((1, tn), lambda i, j: (0, j))]
    if res is not None:
        in_specs.append(pl.BlockSpec((tm, tn), lambda i, j: (i, j)))
    return pl.pallas_call(
        functools.partial(_matmul_kernel, has_res=res is not None),
        grid=(m // tm, n // tn), in_specs=in_specs,
        out_specs=pl.BlockSpec((tm, tn), lambda i, j: (i, j)),
        out_shape=jax.ShapeDtypeStruct((m, n), out_dtype),
        compiler_params=_params("parallel", "arbitrary"), name=name)(*ins)


def _shift_rows(x, sh, row):
    return jnp.where(row >= sh, pltpu.roll(x, sh, axis=0), 0.0)


def _decay_kernel(x_ref, col_ref, row_ref):
    x = x_ref[0]
    c = jnp.minimum(x, 0.0) - jnp.log1p(jnp.exp(-jnp.abs(x)))
    s = x.shape[0]
    row = lax.broadcasted_iota(jnp.int32, x.shape, 0)
    sh = 1
    while sh < s:
        c = c + _shift_rows(c, sh, row)
        sh *= 2
    col_ref[0, 0] = c
    for kb in range(s // ATT_TQ):
        row_ref[0, 0, kb] = c[kb * ATT_TQ:(kb + 1) * ATT_TQ, :].T[0:8, :]


def fox_decay(f_logit_rep, b, s):
    nk = s // ATT_TQ
    return pl.pallas_call(
        _decay_kernel, grid=(b, N_HEADS),
        in_specs=[pl.BlockSpec((1, s, LANES), lambda bi, h: (bi, 0, h))],
        out_specs=[pl.BlockSpec((1, 1, s, LANES), lambda bi, h: (bi, h, 0, 0)),
                   pl.BlockSpec((1, 1, nk, 8, ATT_TQ), lambda bi, h: (bi, h, 0, 0, 0))],
        out_shape=[jax.ShapeDtypeStruct((b, N_HEADS, s, LANES), F32),
                   jax.ShapeDtypeStruct((b, N_HEADS, nk, 8, ATT_TQ), F32)],
        compiler_params=_params("parallel", "parallel"), name="fox_decay")(f_logit_rep)


def _fox_kernel(q_ref, k_ref, v_ref, cc_ref, cr_ref, o_ref):
    tq = ATT_TQ
    i = pl.program_id(2)
    q = q_ref[0]
    cum_q = jnp.concatenate([cc_ref[0, 0]] * (tq // LANES), axis=1)
    qpos = lax.broadcasted_iota(jnp.int32, (tq, tq), 0)
    kpos = lax.broadcasted_iota(jnp.int32, (tq, tq), 1)
    scale = HEAD_DIM ** -0.5

    def body(j, carry):
        m, l, acc = carry
        ks = pl.multiple_of(j * tq, tq)
        k = k_ref[0, pl.ds(ks, tq), :]
        v = v_ref[0, pl.ds(ks, tq), :]
        s = lax.dot_general(q, k, (((1,), (1,)), ((), ())), preferred_element_type=F32) * scale
        s = s + (cum_q - cr_ref[0, 0, j][0:1, :])
        s = jnp.where((j < i) | (kpos <= qpos), s, NEG)
        m_new = jnp.maximum(m, jnp.max(s, axis=1, keepdims=True))
        alpha = jnp.exp(m - m_new)
        p = jnp.exp(s - m_new)
        l = alpha * l + jnp.sum(p, axis=1, keepdims=True)
        acc = alpha * acc + jnp.dot(p.astype(BF16), v, preferred_element_type=F32)
        return m_new, l, acc

    init = (jnp.full((tq, 1), NEG, F32), jnp.zeros((tq, 1), F32), jnp.zeros((tq, HEAD_DIM), F32))
    _, l, acc = lax.fori_loop(0, i + 1, body, init)
    o_ref[0] = (acc / l).astype(o_ref.dtype)


def fox_attention(qkv, cum_col, cum_row):
    b, s, _ = qkv.shape
    nk = s // ATT_TQ
    return pl.pallas_call(
        _fox_kernel, grid=(b, N_HEADS, s // ATT_TQ),
        in_specs=[pl.BlockSpec((1, ATT_TQ, HEAD_DIM), lambda bi, h, i: (bi, i, h)),
                  pl.BlockSpec((1, s, HEAD_DIM), lambda bi, h, i: (bi, 0, N_HEADS + h)),
                  pl.BlockSpec((1, s, HEAD_DIM), lambda bi, h, i: (bi, 0, 2 * N_HEADS + h)),
                  pl.BlockSpec((1, 1, ATT_TQ, LANES), lambda bi, h, i: (bi, h, i, 0)),
                  pl.BlockSpec((1, 1, nk, 8, ATT_TQ), lambda bi, h, i: (bi, h, 0, 0, 0))],
        out_specs=pl.BlockSpec((1, ATT_TQ, HEAD_DIM), lambda bi, h, i: (bi, i, h)),
        out_shape=jax.ShapeDtypeStruct((b, s, MIX_WIDTH), BF16),
        compiler_params=_params("parallel", "parallel", "arbitrary"), name="fox_attention")(
            qkv, qkv, qkv, cum_col, cum_row)


def _chunk_bias_table(rel_bias):
    nkeys = 3 * ATT_TQ
    qa, qi = np.arange(ATT_TQ) // CHUNK, np.arange(ATT_TQ) % CHUNK
    kc, kj = np.arange(nkeys) // CHUNK, np.arange(nkeys) % CHUNK
    band = kc[None, :] - qa[:, None]
    valid = (band >= 0) & (band <= LEFT_CHUNKS)
    dist = (LEFT_CHUNKS * CHUNK + qi[:, None]) - (band * CHUNK + kj[None, :])
    idx = np.where(valid, np.clip(dist, -(CHUNK - 1), REL_CLIP) + (CHUNK - 1), 0)
    tbl = jnp.take(rel_bias, jnp.asarray(idx, jnp.int32), axis=1)
    return jnp.where(jnp.asarray(valid)[None], tbl, NEG).astype(F32)


def _chunk_kernel(q_ref, k_ref, v_ref, bias_ref, o_ref):
    tq = ATT_TQ
    i = pl.program_id(2)
    q = q_ref[0]
    scale = HEAD_DIM ** -0.5
    logits, values = [], []
    for r in range(3):
        kb = i - 2 + r
        ks = pl.multiple_of(jnp.maximum(kb, 0) * tq, tq)
        k = k_ref[0, pl.ds(ks, tq), :]
        values.append(v_ref[0, pl.ds(ks, tq), :])
        s = lax.dot_general(q, k, (((1,), (1,)), ((), ())), preferred_element_type=F32) * scale
        s = s + bias_ref[0, :, r * tq:(r + 1) * tq]
        logits.append(jnp.where(kb >= 0, s, NEG))
    m = jnp.maximum(jnp.maximum(jnp.max(logits[0], axis=1, keepdims=True),
                                jnp.max(logits[1], axis=1, keepdims=True)),
                    jnp.max(logits[2], axis=1, keepdims=True))
    l = jnp.zeros((tq, 1), F32)
    acc = jnp.zeros((tq, HEAD_DIM), F32)
    for r in range(3):
        p = jnp.exp(logits[r] - m)
        l = l + jnp.sum(p, axis=1, keepdims=True)
        acc = acc + jnp.dot(p.astype(BF16), values[r], preferred_element_type=F32)
    o_ref[0] = (acc / l).astype(o_ref.dtype)


def chunk_attention(qkv, bias_tbl):
    b, s, _ = qkv.shape
    return pl.pallas_call(
        _chunk_kernel, grid=(b, N_HEADS, s // ATT_TQ),
        in_specs=[pl.BlockSpec((1, ATT_TQ, HEAD_DIM), lambda bi, h, i: (bi, i, h)),
                  pl.BlockSpec((1, s, HEAD_DIM), lambda bi, h, i: (bi, 0, N_HEADS + h)),
                  pl.BlockSpec((1, s, HEAD_DIM), lambda bi, h, i: (bi, 0, 2 * N_HEADS + h)),
                  pl.BlockSpec((1, ATT_TQ, 3 * ATT_TQ), lambda bi, h, i: (h, 0, 0))],
        out_specs=pl.BlockSpec((1, ATT_TQ, HEAD_DIM), lambda bi, h, i: (bi, i, h)),
        out_shape=jax.ShapeDtypeStruct((b, s, MIX_WIDTH), BF16),
        compiler_params=_params("parallel", "parallel", "arbitrary"), name="chunk_attention")(
            qkv, qkv, qkv, bias_tbl)


def _pool_kernel(x_ref, w_ref, scale_ref, o_ref):
    s = x_ref.shape[1]
    row = lax.broadcasted_iota(jnp.int32, (s, LANES), 0)
    for g, win in enumerate(POOL_WINDOWS):
        cols = slice(g * LANES, (g + 1) * LANES)
        x = x_ref[0, :, cols]
        tot, span = x, 1
        while span < win:
            tot = tot + _shift_rows(tot, span, row)
            span *= 2
        count = jnp.minimum(row + 1, win).astype(F32)
        pooled = tot / count - x
        mixed = jnp.dot(pooled.astype(BF16), w_ref[g], preferred_element_type=F32)
        o_ref[0, :, cols] = (mixed * scale_ref[:, cols]).astype(o_ref.dtype)


def _conv_kernel(h_ref, b_ref, c_ref, w_ref, o_ref):
    s = h_ref.shape[1]
    row = lax.broadcasted_iota(jnp.int32, (s, MIX_WIDTH), 0)
    z = c_ref[0] * h_ref[0]
    acc = w_ref[0:1, :] * _shift_rows(z, 2, row)
    acc = acc + w_ref[1:2, :] * _shift_rows(z, 1, row)
    acc = acc + w_ref[2:3, :] * z
    o_ref[0] = (b_ref[0] * acc).astype(o_ref.dtype)


def pool_and_conv(pc, pool_w, pool_scale, conv_w):
    b, s, _ = pc.shape
    blk = lambda c: pl.BlockSpec((1, s, MIX_WIDTH), lambda bi: (bi, 0, c))
    out = jax.ShapeDtypeStruct((b, s, MIX_WIDTH), BF16)
    y_pool = pl.pallas_call(
        _pool_kernel, grid=(b,),
        in_specs=[blk(0), pl.BlockSpec(pool_w.shape, lambda bi: (0, 0, 0)),
                  pl.BlockSpec((1, MIX_WIDTH), lambda bi: (0, 0))],
        out_specs=blk(0), out_shape=out, compiler_params=_params("parallel"), name="pool")(
            pc, pool_w, pool_scale.reshape(1, MIX_WIDTH))
    y_conv = pl.pallas_call(
        _conv_kernel, grid=(b,),
        in_specs=[blk(1), blk(2), blk(3), pl.BlockSpec((8, MIX_WIDTH), lambda bi: (0, 0))],
        out_specs=blk(0), out_shape=out, compiler_params=_params("parallel"), name="conv")(
            pc, pc, pc, jnp.pad(conv_w, ((0, 8 - CONV_K), (0, 0))))
    return y_pool, y_conv


def _merge_kernel(y0, y1, y2, y3, g0, g1, g2, g3, w_ref, o_ref):
    acc = None
    for n, (y_ref, g_ref) in enumerate(((y0, g0), (y1, g1), (y2, g2), (y3, g3))):
        bd = jnp.dot(y_ref[...], w_ref[n], preferred_element_type=F32)
        term = jax.nn.sigmoid(g_ref[...]) * bd
        acc = term if acc is None else acc + term
    o_ref[...] = acc.astype(o_ref.dtype)


def merge_branches(ys, gate_logit, w_branch):
    t = gate_logit.shape[0]
    tm, tn = MERGE_TM, MERGE_TN
    nj = D_MODEL // tn
    y_spec = pl.BlockSpec((tm, MIX_WIDTH), lambda i, j: (i, 0))
    g_specs = [pl.BlockSpec((tm, tn), functools.partial(lambda i, j, n: (i, n * nj + j), n=n))
               for n in range(N_BRANCH)]
    return pl.pallas_call(
        _merge_kernel, grid=(t // tm, nj),
        in_specs=[y_spec] * N_BRANCH + g_specs
        + [pl.BlockSpec((N_BRANCH, MIX_WIDTH, tn), lambda i, j: (0, 0, j))],
        out_specs=pl.BlockSpec((tm, tn), lambda i, j: (i, j)),
        out_shape=jax.ShapeDtypeStruct((t, D_MODEL), BF16),
        compiler_params=_params("parallel", "arbitrary"), name="merge")(
            *ys, *([gate_logit] * N_BRANCH), w_branch)


def _top16_rows(x):
    n = x.shape[0]
    rows = lax.broadcasted_iota(jnp.int32, x.shape, 0)
    vals = []
    for _ in range(PEER_TOPK):
        m = jnp.max(x, axis=0, keepdims=True)
        first = jnp.min(jnp.where(x == m, rows, n), axis=0, keepdims=True)
        vals.append(m)
        x = jnp.where(rows == first, -jnp.inf, x)
    return jnp.concatenate(vals, axis=0)


def _peer_select_kernel(q_ref, keys_ref, s1_ref, s2_ref, e1_ref, e2_ref, tau_ref):
    taus = []
    for h in range(PEER_HEADS):
        halves = []
        for p in range(2):
            c0 = (2 * h + p) * PEER_KEYS
            halves.append(lax.dot_general(keys_ref[h, p], q_ref[:, c0:c0 + PEER_KEYS],
                                          (((1,), (1,)), ((), ())), preferred_element_type=F32))
        s1, s2 = halves
        t1, t2 = _top16_rows(s1), _top16_rows(s2)
        cand = jnp.concatenate([t1[a:a + 1] + t2 for a in range(PEER_TOPK)], axis=0)
        best = _top16_rows(cand)
        norm = jnp.sum(jnp.exp(best - best[0:1]), axis=0, keepdims=True)
        s1_ref[h] = s1
        s2_ref[h] = s2
        e1_ref[h] = jnp.exp(s1 - t1[0:1]) / norm
        e2_ref[h] = jnp.exp(s2 - t2[0:1])
        taus.append(best[PEER_TOPK - 1:PEER_TOPK])
    tau_ref[...] = jnp.concatenate(taus, axis=0)


def peer_select(q, keys):
    t = q.shape[0]
    big = pl.BlockSpec((PEER_HEADS, PEER_KEYS, SEL_TS), lambda i: (0, 0, i))
    big_shape = jax.ShapeDtypeStruct((PEER_HEADS, PEER_KEYS, t), F32)
    return pl.pallas_call(
        _peer_select_kernel, grid=(t // SEL_TS,),
        in_specs=[pl.BlockSpec((SEL_TS, q.shape[1]), lambda i: (i, 0)),
                  pl.BlockSpec(keys.shape, lambda i: (0, 0, 0, 0))],
        out_specs=[big] * 4 + [pl.BlockSpec((PEER_HEADS, SEL_TS), lambda i: (0, i))],
        out_shape=[big_shape] * 4 + [jax.ShapeDtypeStruct((PEER_HEADS, t), F32)],
        compiler_params=_params("parallel"), name="peer_select")(q, keys)


def _peer_dense_kernel(x_ref, u_ref, v_ref, s1_ref, s2_ref, e1_ref, e2_ref, tau_ref, o_ref):
    e = pl.program_id(1)
    nsub = PEER_EB // PEER_KEYS

    @pl.when(e == 0)
    def _():
        o_ref[...] = jnp.zeros_like(o_ref)

    act = jax.nn.gelu(lax.dot_general(u_ref[...], x_ref[...], (((1,), (1,)), ((), ())),
                                      preferred_element_type=F32))
    parts = []
    for ii in range(nsub):
        i = e * nsub + ii
        gate = None
        for h in range(PEER_HEADS):
            z = s2_ref[h] + s1_ref[h, pl.ds(i, 1), :]
            w = e2_ref[h] * e1_ref[h, pl.ds(i, 1), :]
            term = jnp.where(z >= tau_ref[h:h + 1, :], w, 0.0)
            gate = term if gate is None else gate + term
        parts.append(act[ii * PEER_KEYS:(ii + 1) * PEER_KEYS] * gate)
    wt = jnp.concatenate(parts, axis=0)
    o_ref[...] += jnp.dot(wt.T.astype(BF16), v_ref[...], preferred_element_type=F32)


def peer_dense(xn, u, v, s1, s2, e1, e2, tau):
    t = xn.shape[0]
    big = pl.BlockSpec((PEER_HEADS, PEER_KEYS, PEER_TM), lambda i, e: (0, 0, i))
    wblk = pl.BlockSpec((PEER_EB, D_MODEL), lambda i, e: (e, 0))
    return pl.pallas_call(
        _peer_dense_kernel, grid=(t // PEER_TM, PEER_EXPERTS // PEER_EB),
        in_specs=[pl.BlockSpec((PEER_TM, D_MODEL), lambda i, e: (i, 0)), wblk, wblk,
                  big, big, big, big, pl.BlockSpec((PEER_HEADS, PEER_TM), lambda i, e: (0, i))],
        out_specs=pl.BlockSpec((PEER_TM, D_MODEL), lambda i, e: (i, 0)),
        out_shape=jax.ShapeDtypeStruct((t, D_MODEL), F32),
        compiler_params=_params("parallel", "arbitrary"), name="peer_dense")(
            xn, u, v, s1, s2, e1, e2, tau)


def kernel(x, norm_mix, w_in, b_in, pool_w, pool_scale, conv_w, rel_bias, w_branch, w_out, norm_ffn,
           peer_wq, peer_keys, peer_u, peer_v, norm_final):
    b, s, d = x.shape
    t = b * s
    depth = w_in.shape[0]
    o_pc = FORGET_OFFSET + N_HEADS
    o_cq = o_pc + 4 * MIX_WIDTH
    o_gate = o_cq + 3 * MIX_WIDTH

    h = x.reshape(t, d)
    y_peer = None
    for l in range(depth):
        if y_peer is None:
            xn = rmsnorm(h, norm_mix[l], BF16)
        else:
            h, xn = rmsnorm(h, norm_mix[l], BF16, add=y_peer, emit_sum=True)
        w, bias = w_in[l].astype(BF16), b_in[l]
        proj = lambda c0, c1, dt, name: matmul(xn, w[:, c0:c1], bias[c0:c1], dt, name=name)
        fox_qkv = proj(0, FORGET_OFFSET, BF16, "proj_fox").reshape(b, s, -1)
        f_rep = matmul(xn, jnp.repeat(w[:, FORGET_OFFSET:o_pc], LANES, axis=1),
                       jnp.repeat(bias[FORGET_OFFSET:o_pc], LANES), F32, name="proj_forget").reshape(b, s, -1)
        pc = proj(o_pc, o_cq, F32, "proj_pool_conv").reshape(b, s, -1)
        chunk_qkv = proj(o_cq, o_gate, BF16, "proj_chunk").reshape(b, s, -1)
        gate_logit = proj(o_gate, w.shape[1], F32, "proj_gate")

        cum_col, cum_row = fox_decay(f_rep, b, s)
        y_fox = fox_attention(fox_qkv, cum_col, cum_row)
        y_pool, y_conv = pool_and_conv(pc, pool_w[l].astype(BF16), pool_scale[l], conv_w[l])
        y_chunk = chunk_attention(chunk_qkv, _chunk_bias_table(rel_bias[l]))

        ys = [y.reshape(t, MIX_WIDTH) for y in (y_fox, y_pool, y_conv, y_chunk)]
        merged = merge_branches(ys, gate_logit, w_branch[l].astype(BF16))
        h = matmul(merged, w_out[l].astype(BF16), jnp.zeros((d,), F32), F32, res=h, name="out_proj")

        hn = rmsnorm(h, norm_ffn[l], BF16)
        q = matmul(hn, peer_wq[l].astype(BF16), jnp.zeros((peer_wq.shape[2],), F32), BF16, name="peer_query")
        s1, s2, e1, e2, tau = peer_select(q, peer_keys[l].astype(BF16))
        y_peer = peer_dense(hn, peer_u[l].astype(BF16), peer_v[l].astype(BF16), s1, s2, e1, e2, tau)

    out = rmsnorm(h, norm_final, F32, add=y_peer)
    return out.reshape(b, s, d)
```

```python
import functools

import numpy as np
import jax
import jax.numpy as jnp
from jax import lax
from jax.experimental import pallas as pl
from jax.experimental.pallas import tpu as pltpu

F32 = jnp.float32
BF16 = jnp.bfloat16

D_MODEL = 2048
HEAD_DIM = 128
MIX_WIDTH = D_MODEL // 4
N_BRANCH = 4
N_HEADS = MIX_WIDTH // HEAD_DIM
POOL_WINDOWS = (2, 4, 8, 16)
CONV_K = 3
CHUNK = 64
LEFT_CHUNKS = 8
REL_CLIP = 256
PEER_HEADS = 8
PEER_KEYS = 128
PEER_EXPERTS = PEER_KEYS * PEER_KEYS
PEER_TOPK = 16
EPS = 1e-6
FORGET_OFFSET = 3 * MIX_WIDTH

LANES = 128
BF16_ROWS = 16
VMEM_LIMIT_BYTES = 56 * 2**20

NEG = -1e30

NORM_TM = 512
MM_TM, MM_TN = 1024, 512
ATT_TQ = 256
MERGE_TM, MERGE_TN = 512, 512
SEL_TS = 128
PEER_TM, PEER_EB = 512, 1024

_G_FOX, _G_PC, _G_CHUNK, _G_GATE, _G_FORGET = 0, 3, 7, 10, 26
_G_END = 27


def _params(*sem):
    return pltpu.CompilerParams(dimension_semantics=sem, vmem_limit_bytes=VMEM_LIMIT_BYTES)


def _norm_kernel(*refs, has_add, emit_sum):
    if has_add:
        x_ref, y_ref, g_ref = refs[:3]
        x = x_ref[...] + y_ref[...]
    else:
        x_ref, g_ref = refs[:2]
        x = x_ref[...]
    outs = refs[3 if has_add else 2:]
    if emit_sum:
        outs[0][...] = x
    inv = lax.rsqrt(jnp.mean(x * x, axis=-1, keepdims=True) + EPS)
    outs[-1][...] = ((x * inv) * g_ref[...]).astype(outs[-1].dtype)


def rmsnorm(x, gains, l, out_dtype, add=None, emit_sum=False):
    t, d = x.shape
    row = pl.BlockSpec((NORM_TM, d), lambda i: (i, 0))
    ins = [x] + ([add] if add is not None else []) + [gains]
    in_specs = [row] * (len(ins) - 1) + [pl.BlockSpec((None, 1, d), lambda i: (l, 0, 0))]
    out_shape = [jax.ShapeDtypeStruct((t, d), out_dtype)]
    if emit_sum:
        out_shape = [jax.ShapeDtypeStruct((t, d), F32)] + out_shape
    res = pl.pallas_call(
        functools.partial(_norm_kernel, has_add=add is not None, emit_sum=emit_sum),
        grid=(t // NORM_TM,), in_specs=in_specs, out_specs=[row] * len(out_shape),
        out_shape=out_shape, compiler_params=_params("parallel"), name="rmsnorm")(*ins)
    return res if emit_sum else res[0]


def _matmul_kernel(*refs, has_bias, has_res):
    a_ref, b_ref = refs[:2]
    o_ref = refs[-1]
    acc = jnp.dot(a_ref[...], b_ref[...], preferred_element_type=F32)
    if has_bias:
        acc = acc + refs[2][...]
    if has_res:
        acc = acc + refs[-2][...]
    o_ref[...] = acc.astype(o_ref.dtype)


def matmul(a, b, l, out_dtype, bias=None, res=None, col_blocks=None, name="matmul"):
    m, k = a.shape
    tm, tn = min(MM_TM, m), MM_TN
    first, count = col_blocks if col_blocks is not None else (0, b.shape[2] // tn)
    ins = [a, b]
    in_specs = [pl.BlockSpec((tm, k), lambda i, j: (i, 0)),
                pl.BlockSpec((None, k, tn), lambda i, j: (l, 0, first + j))]
    if bias is not None:
        ins.append(bias)
        in_specs.append(pl.BlockSpec((None, 1, tn), lambda i, j: (l, 0, first + j)))
    if res is not None:
        ins.append(res)
        in_specs.append(pl.BlockSpec((tm, tn), lambda i, j: (i, j)))
    return pl.pallas_call(
        functools.partial(_matmul_kernel, has_bias=bias is not None, has_res=res is not None),
        grid=(m // tm, count), in_specs=in_specs,
        out_specs=pl.BlockSpec((tm, tn), lambda i, j: (i, j)),
        out_shape=jax.ShapeDtypeStruct((m, count * tn), out_dtype),
        compiler_params=_params("parallel", "arbitrary"), name=name)(*ins)


def _shift_rows(x, sh, row):
    return jnp.where(row >= sh, pltpu.roll(x, sh, axis=0), 0.0)


def _decay_kernel(x_ref, col_ref, row_ref):
    x = x_ref[0]
    c = jnp.minimum(x, 0.0) - jnp.log1p(jnp.exp(-jnp.abs(x)))
    s = x.shape[0]
    row = lax.broadcasted_iota(jnp.int32, x.shape, 0)
    sh = 1
    while sh < s:
        c = c + _shift_rows(c, sh, row)
        sh *= 2
    col_ref[0, 0] = c
    for kb in range(s // ATT_TQ):
        row_ref[0, 0, kb] = c[kb * ATT_TQ:(kb + 1) * ATT_TQ, :].T[0:8, :]


def fox_decay(f_logit_rep, b, s):
    nk = s // ATT_TQ
    return pl.pallas_call(
        _decay_kernel, grid=(b, N_HEADS),
        in_specs=[pl.BlockSpec((1, s, LANES), lambda bi, h: (bi, 0, h))],
        out_specs=[pl.BlockSpec((1, 1, s, LANES), lambda bi, h: (bi, h, 0, 0)),
                   pl.BlockSpec((1, 1, nk, 8, ATT_TQ), lambda bi, h: (bi, h, 0, 0, 0))],
        out_shape=[jax.ShapeDtypeStruct((b, N_HEADS, s, LANES), F32),
                   jax.ShapeDtypeStruct((b, N_HEADS, nk, 8, ATT_TQ), F32)],
        compiler_params=_params("parallel", "parallel"), name="fox_decay")(f_logit_rep)


def _fox_kernel(q_ref, k_ref, v_ref, cc_ref, cr_ref, o_ref):
    tq = ATT_TQ
    i = pl.program_id(2)
    q = q_ref[0]
    cum_q = jnp.concatenate([cc_ref[0, 0]] * (tq // LANES), axis=1)
    qpos = lax.broadcasted_iota(jnp.int32, (tq, tq), 0)
    kpos = lax.broadcasted_iota(jnp.int32, (tq, tq), 1)
    scale = HEAD_DIM ** -0.5

    def body(j, carry):
        m, l, acc = carry
        ks = pl.multiple_of(j * tq, tq)
        k = k_ref[0, pl.ds(ks, tq), :]
        v = v_ref[0, pl.ds(ks, tq), :]
        s = lax.dot_general(q, k, (((1,), (1,)), ((), ())), preferred_element_type=F32) * scale
        s = s + (cum_q - cr_ref[0, 0, j][0:1, :])
        s = jnp.where((j < i) | (kpos <= qpos), s, NEG)
        m_new = jnp.maximum(m, jnp.max(s, axis=1, keepdims=True))
        alpha = jnp.exp(m - m_new)
        p = jnp.exp(s - m_new)
        l = alpha * l + jnp.sum(p, axis=1, keepdims=True)
        acc = alpha * acc + jnp.dot(p.astype(BF16), v, preferred_element_type=F32)
        return m_new, l, acc

    init = (jnp.full((tq, 1), NEG, F32), jnp.zeros((tq, 1), F32), jnp.zeros((tq, HEAD_DIM), F32))
    _, l, acc = lax.fori_loop(0, i + 1, body, init)
    o_ref[0] = (acc / l).astype(o_ref.dtype)


def fox_attention(qkv, cum_col, cum_row):
    b, s, _ = qkv.shape
    nk = s // ATT_TQ
    return pl.pallas_call(
        _fox_kernel, grid=(b, N_HEADS, s // ATT_TQ),
        in_specs=[pl.BlockSpec((1, ATT_TQ, HEAD_DIM), lambda bi, h, i: (bi, i, h)),
                  pl.BlockSpec((1, s, HEAD_DIM), lambda bi, h, i: (bi, 0, N_HEADS + h)),
                  pl.BlockSpec((1, s, HEAD_DIM), lambda bi, h, i: (bi, 0, 2 * N_HEADS + h)),
                  pl.BlockSpec((1, 1, ATT_TQ, LANES), lambda bi, h, i: (bi, h, i, 0)),
                  pl.BlockSpec((1, 1, nk, 8, ATT_TQ), lambda bi, h, i: (bi, h, 0, 0, 0))],
        out_specs=pl.BlockSpec((1, ATT_TQ, HEAD_DIM), lambda bi, h, i: (bi, i, h)),
        out_shape=jax.ShapeDtypeStruct((b, s, MIX_WIDTH), BF16),
        compiler_params=_params("parallel", "parallel", "arbitrary"), name="fox_attention")(
            qkv, qkv, qkv, cum_col, cum_row)


_TOEPLITZ_W = 4 * ATT_TQ


def _bias_table_kernel(g_ref, o_ref):
    nkeys = 3 * ATT_TQ
    rows = jnp.broadcast_to(g_ref[0], (ATT_TQ, _TOEPLITZ_W))
    toep = pltpu.roll(rows, 0, 1, stride=1, stride_axis=0)[:, :nkeys]
    qc = lax.broadcasted_iota(jnp.int32, (ATT_TQ, nkeys), 0) // CHUNK
    kc = lax.broadcasted_iota(jnp.int32, (ATT_TQ, nkeys), 1) // CHUNK
    band = kc - qc
    o_ref[0] = jnp.where((band >= 0) & (band <= LEFT_CHUNKS), toep, NEG)


def chunk_bias_table(rel_bias):
    nl, nh, rel = rel_bias.shape
    far = rel_bias[:, :, rel - 1:]
    n_far = 2 * ATT_TQ - REL_CLIP + 1
    n_near = 3 * ATT_TQ - n_far - (rel - 1)
    g = jnp.concatenate([jnp.broadcast_to(far, (nl, nh, n_far)),
                         rel_bias[:, :, rel - 2::-1],
                         jnp.broadcast_to(rel_bias[:, :, :1], (nl, nh, n_near)),
                         jnp.broadcast_to(far, (nl, nh, _TOEPLITZ_W - 3 * ATT_TQ))], axis=2)
    g = g.reshape(nl * nh, 1, _TOEPLITZ_W)
    return pl.pallas_call(
        _bias_table_kernel, grid=(nl * nh,),
        in_specs=[pl.BlockSpec((1, 1, _TOEPLITZ_W), lambda i: (i, 0, 0))],
        out_specs=pl.BlockSpec((1, ATT_TQ, 3 * ATT_TQ), lambda i: (i, 0, 0)),
        out_shape=jax.ShapeDtypeStruct((nl * nh, ATT_TQ, 3 * ATT_TQ), F32),
        compiler_params=_params("parallel"), name="chunk_bias_table")(g)


def _chunk_kernel(q_ref, k_ref, v_ref, bias_ref, o_ref):
    tq = ATT_TQ
    i = pl.program_id(2)
    q = q_ref[0]
    scale = HEAD_DIM ** -0.5
    logits, values = [], []
    for r in range(3):
        kb = i - 2 + r
        ks = pl.multiple_of(jnp.maximum(kb, 0) * tq, tq)
        k = k_ref[0, pl.ds(ks, tq), :]
        values.append(v_ref[0, pl.ds(ks, tq), :])
        s = lax.dot_general(q, k, (((1,), (1,)), ((), ())), preferred_element_type=F32) * scale
        s = s + bias_ref[0, :, r * tq:(r + 1) * tq]
        logits.append(jnp.where(kb >= 0, s, NEG))
    m = jnp.maximum(jnp.maximum(jnp.max(logits[0], axis=1, keepdims=True),
                                jnp.max(logits[1], axis=1, keepdims=True)),
                    jnp.max(logits[2], axis=1, keepdims=True))
    l = jnp.zeros((tq, 1), F32)
    acc = jnp.zeros((tq, HEAD_DIM), F32)
    for r in range(3):
        p = jnp.exp(logits[r] - m)
        l = l + jnp.sum(p, axis=1, keepdims=True)
        acc = acc + jnp.dot(p.astype(BF16), values[r], preferred_element_type=F32)
    o_ref[0] = (acc / l).astype(o_ref.dtype)


def chunk_attention(qkv, bias_tbl, l):
    b, s, _ = qkv.shape
    return pl.pallas_call(
        _chunk_kernel, grid=(b, N_HEADS, s // ATT_TQ),
        in_specs=[pl.BlockSpec((1, ATT_TQ, HEAD_DIM), lambda bi, h, i: (bi, i, h)),
                  pl.BlockSpec((1, s, HEAD_DIM), lambda bi, h, i: (bi, 0, N_HEADS + h)),
                  pl.BlockSpec((1, s, HEAD_DIM), lambda bi, h, i: (bi, 0, 2 * N_HEADS + h)),
                  pl.BlockSpec((1, ATT_TQ, 3 * ATT_TQ), lambda bi, h, i: (l * N_HEADS + h, 0, 0))],
        out_specs=pl.BlockSpec((1, ATT_TQ, HEAD_DIM), lambda bi, h, i: (bi, i, h)),
        out_shape=jax.ShapeDtypeStruct((b, s, MIX_WIDTH), BF16),
        compiler_params=_params("parallel", "parallel", "arbitrary"), name="chunk_attention")(
            qkv, qkv, qkv, bias_tbl)


def _pool_kernel(x_ref, w_ref, scale_ref, o_ref):
    s = x_ref.shape[1]
    row = lax.broadcasted_iota(jnp.int32, (s, LANES), 0)
    for g, win in enumerate(POOL_WINDOWS):
        cols = slice(g * LANES, (g + 1) * LANES)
        x = x_ref[0, :, cols]
        tot, span = x, 1
        while span < win:
            tot = tot + _shift_rows(tot, span, row)
            span *= 2
        count = jnp.minimum(row + 1, win).astype(F32)
        pooled = tot / count - x
        mixed = jnp.dot(pooled.astype(BF16), w_ref[g], preferred_element_type=F32)
        o_ref[0, :, cols] = (mixed * scale_ref[:, cols]).astype(o_ref.dtype)


def _conv_kernel(h_ref, b_ref, c_ref, w_ref, o_ref):
    s = h_ref.shape[1]
    row = lax.broadcasted_iota(jnp.int32, (s, MIX_WIDTH), 0)
    z = c_ref[0] * h_ref[0]
    acc = w_ref[0:1, :] * _shift_rows(z, 2, row)
    acc = acc + w_ref[1:2, :] * _shift_rows(z, 1, row)
    acc = acc + w_ref[2:3, :] * z
    o_ref[0] = (b_ref[0] * acc).astype(o_ref.dtype)


def pool_and_conv(pc, pool_w, pool_scale, conv_w, l):
    b, s, _ = pc.shape
    blk = lambda c: pl.BlockSpec((1, s, MIX_WIDTH), lambda bi: (bi, 0, c))
    out = jax.ShapeDtypeStruct((b, s, MIX_WIDTH), BF16)
    y_pool = pl.pallas_call(
        _pool_kernel, grid=(b,),
        in_specs=[blk(0), pl.BlockSpec((None,) + pool_w.shape[1:], lambda bi: (l, 0, 0, 0)),
                  pl.BlockSpec((None, 1, MIX_WIDTH), lambda bi: (l, 0, 0))],
        out_specs=blk(0), out_shape=out, compiler_params=_params("parallel"), name="pool")(
            pc, pool_w, pool_scale)
    y_conv = pl.pallas_call(
        _conv_kernel, grid=(b,),
        in_specs=[blk(1), blk(2), blk(3), pl.BlockSpec((None, 8, MIX_WIDTH), lambda bi: (l, 0, 0))],
        out_specs=blk(0), out_shape=out, compiler_params=_params("parallel"), name="conv")(
            pc, pc, pc, conv_w)
    return y_pool, y_conv


def _merge_kernel(y0, y1, y2, y3, g0, g1, g2, g3, w_ref, o_ref):
    acc = None
    for n, (y_ref, g_ref) in enumerate(((y0, g0), (y1, g1), (y2, g2), (y3, g3))):
        bd = jnp.dot(y_ref[...], w_ref[n], preferred_element_type=F32)
        term = jax.nn.sigmoid(g_ref[...]) * bd
        acc = term if acc is None else acc + term
    o_ref[...] = acc.astype(o_ref.dtype)


def merge_branches(ys, gate_logit, w_branch, l):
    t = gate_logit.shape[0]
    tm, tn = MERGE_TM, MERGE_TN
    nj = D_MODEL // tn
    y_spec = pl.BlockSpec((tm, MIX_WIDTH), lambda i, j: (i, 0))
    g_specs = [pl.BlockSpec((tm, tn), functools.partial(lambda i, j, n: (i, n * nj + j), n=n))
               for n in range(N_BRANCH)]
    return pl.pallas_call(
        _merge_kernel, grid=(t // tm, nj),
        in_specs=[y_spec] * N_BRANCH + g_specs
        + [pl.BlockSpec((None, N_BRANCH, MIX_WIDTH, tn), lambda i, j: (l, 0, 0, j))],
        out_specs=pl.BlockSpec((tm, tn), lambda i, j: (i, j)),
        out_shape=jax.ShapeDtypeStruct((t, D_MODEL), BF16),
        compiler_params=_params("parallel", "arbitrary"), name="merge")(
            *ys, *([gate_logit] * N_BRANCH), w_branch)


_CAND_PER_ROW = tuple(PEER_TOPK // (a + 1) for a in range(PEER_TOPK))
_CAND_ROWS = -(-sum(_CAND_PER_ROW) // 8) * 8


def _extract16(x, rank_fill):
    n = x.shape[0]
    rows = lax.broadcasted_iota(jnp.int32, x.shape, 0).astype(F32)
    rank = None if rank_fill is None else jnp.full(x.shape, rank_fill, F32)
    vals = []
    for r in range(PEER_TOPK):
        m = jnp.max(x, axis=0, keepdims=True)
        first = jnp.min(jnp.where(x == m, rows, float(n)), axis=0, keepdims=True)
        hit = rows == first
        vals.append(m)
        x = jnp.where(hit, -jnp.inf, x)
        if rank is not None:
            rank = jnp.where(hit, float(r), rank)
    return jnp.concatenate(vals, axis=0), x, rank


def _peer_select_kernel(q_ref, keys_ref, r2_ref, e2_ref, c1_ref, e1_ref):
    ts = q_ref.shape[0]
    cand_start = np.concatenate([[0], np.cumsum(_CAND_PER_ROW)])
    n_cand = int(cand_start[-1])
    pad = _CAND_ROWS - n_cand
    is_cand = lax.broadcasted_iota(jnp.int32, (_CAND_ROWS, ts), 0) < n_cand
    for h in range(PEER_HEADS):
        halves = []
        for p in range(2):
            c0 = (2 * h + p) * PEER_KEYS
            halves.append(lax.dot_general(keys_ref[h, p], q_ref[:, c0:c0 + PEER_KEYS],
                                          (((1,), (1,)), ((), ())), preferred_element_type=F32))
        s1, s2 = halves
        t1, _, rank1 = _extract16(s1, float(PEER_TOPK))
        t2, _, rank2 = _extract16(s2, float(PEER_TOPK))
        cand = jnp.concatenate([t1[a:a + 1] + t2[0:n] for a, n in enumerate(_CAND_PER_ROW)]
                               + [jnp.full((pad, ts), -jnp.inf, F32)], axis=0)
        _, left, _ = _extract16(cand, None)
        taken = (left == -jnp.inf) & is_cand
        norm = jnp.sum(jnp.where(taken, jnp.exp(cand - cand[0:1]), 0.0), axis=0, keepdims=True)
        taken_f = jnp.where(taken, 1.0, 0.0)
        count1 = jnp.zeros_like(s1)
        for a in range(PEER_TOPK):
            c_a = jnp.sum(taken_f[cand_start[a]:cand_start[a + 1]], axis=0, keepdims=True)
            count1 = jnp.where(rank1 == float(a), c_a, count1)
        r2_ref[h] = rank2.astype(r2_ref.dtype)
        e2_ref[h] = jnp.exp(s2 - t2[0:1]).astype(e2_ref.dtype)
        c1_ref[h] = count1
        e1_ref[h] = jnp.exp(s1 - t1[0:1]) / norm


def peer_select(q, keys, l):
    t = q.shape[0]
    big = pl.BlockSpec((PEER_HEADS, PEER_KEYS, SEL_TS), lambda i: (0, 0, i))
    shape = lambda dt: jax.ShapeDtypeStruct((PEER_HEADS, PEER_KEYS, t), dt)
    return pl.pallas_call(
        _peer_select_kernel, grid=(t // SEL_TS,),
        in_specs=[pl.BlockSpec((SEL_TS, q.shape[1]), lambda i: (i, 0)),
                  pl.BlockSpec((None,) + keys.shape[1:], lambda i: (l, 0, 0, 0, 0))],
        out_specs=[big] * 4, out_shape=[shape(BF16), shape(BF16), shape(F32), shape(F32)],
        compiler_params=_params("parallel"), name="peer_select")(q, keys)


def _expert_gate_weights(act, i0, r2_ref, e2_ref, c1_ref, e1_ref):
    tm = act.shape[1]
    parts = []
    for ii in range(act.shape[0] // PEER_KEYS):
        i = i0 + ii
        gate = [None] * (PEER_KEYS // BF16_ROWS)
        for h in range(PEER_HEADS):
            count = jnp.broadcast_to(c1_ref[h, pl.ds(i, 1), :], (BF16_ROWS, tm)).astype(BF16)
            e1 = jnp.broadcast_to(e1_ref[h, pl.ds(i, 1), :], (BF16_ROWS, tm)).astype(BF16)
            for jj in range(len(gate)):
                rows = slice(jj * BF16_ROWS, (jj + 1) * BF16_ROWS)
                term = jnp.where(r2_ref[h, rows, :] < count, e2_ref[h, rows, :] * e1, 0.0)
                gate[jj] = term if gate[jj] is None else gate[jj] + term
        for jj, g in enumerate(gate):
            r0 = ii * PEER_KEYS + jj * BF16_ROWS
            parts.append(act[r0:r0 + BF16_ROWS] * g.astype(F32))
    return jnp.concatenate(parts, axis=0)


def _peer_dense_kernel(x_ref, u_ref, v_ref, r2_ref, e2_ref, c1_ref, e1_ref, o_ref):
    e = pl.program_id(1)

    @pl.when(e == 0)
    def _():
        o_ref[...] = jnp.zeros_like(o_ref)

    act = jax.nn.gelu(lax.dot_general(u_ref[...], x_ref[...], (((1,), (1,)), ((), ())),
                                      preferred_element_type=F32))
    wt = _expert_gate_weights(act, e * (PEER_EB // PEER_KEYS), r2_ref, e2_ref, c1_ref, e1_ref)
    o_ref[...] += jnp.dot(wt.T.astype(BF16), v_ref[...], preferred_element_type=F32)


def peer_dense(xn, u, v, sel, l):
    t = xn.shape[0]
    big = pl.BlockSpec((PEER_HEADS, PEER_KEYS, PEER_TM), lambda i, e: (0, 0, i))
    wblk = pl.BlockSpec((None, PEER_EB, D_MODEL), lambda i, e: (l, e, 0))
    return pl.pallas_call(
        _peer_dense_kernel, grid=(t // PEER_TM, PEER_EXPERTS // PEER_EB),
        in_specs=[pl.BlockSpec((PEER_TM, D_MODEL), lambda i, e: (i, 0)), wblk, wblk, big, big, big, big],
        out_specs=pl.BlockSpec((PEER_TM, D_MODEL), lambda i, e: (i, 0)),
        out_shape=jax.ShapeDtypeStruct((t, D_MODEL), F32),
        compiler_params=_params("parallel", "arbitrary"), name="peer_dense")(xn, u, v, *sel)


def kernel(x, norm_mix, w_in, b_in, pool_w, pool_scale, conv_w, rel_bias, w_branch, w_out, norm_ffn,
           peer_wq, peer_keys, peer_u, peer_v, norm_final):
    b, s, d = x.shape
    t = b * s
    depth = w_in.shape[0]
    o_pc = FORGET_OFFSET + N_HEADS

    rearrange = lambda a: jnp.concatenate(
        [a[..., :FORGET_OFFSET], a[..., o_pc:], jnp.repeat(a[..., FORGET_OFFSET:o_pc], LANES, axis=-1)], axis=-1)
    w_all = rearrange(w_in).astype(BF16)
    b_all = rearrange(b_in).reshape(depth, 1, -1)
    w_branch_b, w_out_b, wq_b = w_branch.astype(BF16), w_out.astype(BF16), peer_wq.astype(BF16)
    u_b, v_b, keys_b, pool_w_b = peer_u.astype(BF16), peer_v.astype(BF16), peer_keys.astype(BF16), pool_w.astype(BF16)
    pool_scale_r = pool_scale.reshape(depth, 1, MIX_WIDTH)
    conv_w_r = jnp.pad(conv_w, ((0, 0), (0, 8 - CONV_K), (0, 0)))
    g_mix, g_ffn = norm_mix.reshape(depth, 1, d), norm_ffn.reshape(depth, 1, d)
    bias_tbl = chunk_bias_table(rel_bias)

    h = x.reshape(t, d)
    y_peer = None
    for l in range(depth):
        if y_peer is None:
            xn = rmsnorm(h, g_mix, l, BF16)
        else:
            h, xn = rmsnorm(h, g_mix, l, BF16, add=y_peer, emit_sum=True)
        proj = lambda g0, g1, dt, name: matmul(xn, w_all, l, dt, bias=b_all, col_blocks=(g0, g1 - g0), name=name)
        fox_qkv = proj(_G_FOX, _G_PC, BF16, "proj_fox").reshape(b, s, -1)
        pc = proj(_G_PC, _G_CHUNK, F32, "proj_pool_conv").reshape(b, s, -1)
        chunk_qkv = proj(_G_CHUNK, _G_GATE, BF16, "proj_chunk").reshape(b, s, -1)
        gate_logit = proj(_G_GATE, _G_FORGET, F32, "proj_gate")
        f_rep = proj(_G_FORGET, _G_END, F32, "proj_forget").reshape(b, s, -1)

        cum_col, cum_row = fox_decay(f_rep, b, s)
        y_fox = fox_attention(fox_qkv, cum_col, cum_row)
        y_pool, y_conv = pool_and_conv(pc, pool_w_b, pool_scale_r, conv_w_r, l)
        y_chunk = chunk_attention(chunk_qkv, bias_tbl, l)

        ys = [y.reshape(t, MIX_WIDTH) for y in (y_fox, y_pool, y_conv, y_chunk)]
        merged = merge_branches(ys, gate_logit, w_branch_b, l)
        h = matmul(merged, w_out_b, l, F32, res=h, name="out_proj")

        hn = rmsnorm(h, g_ffn, l, BF16)
        q = matmul(hn, wq_b, l, BF16, name="peer_query")
        sel = peer_select(q, keys_b, l)
        y_peer = peer_dense(hn, u_b, v_b, sel, l)

    out = rmsnorm(h, norm_final.reshape(1, 1, d), 0, F32, add=y_peer)
    return out.reshape(b, s, d)
```

```python
import functools

import numpy as np
import jax
import jax.numpy as jnp
from jax import lax
from jax.experimental import pallas as pl
from jax.experimental.pallas import tpu as pltpu

F32 = jnp.float32
BF16 = jnp.bfloat16

D_MODEL = 2048
HEAD_DIM = 128
MIX_WIDTH = D_MODEL // 4
N_BRANCH = 4
N_HEADS = MIX_WIDTH // HEAD_DIM
POOL_WINDOWS = (2, 4, 8, 16)
CONV_K = 3
CHUNK = 64
LEFT_CHUNKS = 8
REL_CLIP = 256
PEER_HEADS = 8
PEER_KEYS = 128
PEER_EXPERTS = PEER_KEYS * PEER_KEYS
PEER_TOPK = 16
EPS = 1e-6
FORGET_OFFSET = 3 * MIX_WIDTH

LANES = 128
BF16_ROWS = 16
VMEM_LIMIT_BYTES = 56 * 2**20

NEG = -1e30

NORM_TM = 512
MM_TM, MM_TN = 1024, 512
ATT_TQ = 256
FOX_TQ = 512
MERGE_TM, MERGE_TN = 1024, 512
SEL_TS = 256
PEER_TM, PEER_EB = 512, 1024

_G_PC, _G_CHUNK, _G_GATE = 0, 4, 7


def _params(*sem):
    return pltpu.CompilerParams(dimension_semantics=sem, vmem_limit_bytes=VMEM_LIMIT_BYTES)


def _norm_kernel(*refs, has_add, emit_sum):
    if has_add:
        x_ref, y_ref, g_ref = refs[:3]
        x = x_ref[...] + y_ref[...]
    else:
        x_ref, g_ref = refs[:2]
        x = x_ref[...]
    outs = refs[3 if has_add else 2:]
    if emit_sum:
        outs[0][...] = x
    inv = lax.rsqrt(jnp.mean(x * x, axis=-1, keepdims=True) + EPS)
    outs[-1][...] = ((x * inv) * g_ref[...]).astype(outs[-1].dtype)


def rmsnorm(x, gains, l, out_dtype, add=None, emit_sum=False):
    t, d = x.shape
    row = pl.BlockSpec((NORM_TM, d), lambda i: (i, 0))
    ins = [x] + ([add] if add is not None else []) + [gains]
    in_specs = [row] * (len(ins) - 1) + [pl.BlockSpec((None, 1, d), lambda i: (l, 0, 0))]
    out_shape = [jax.ShapeDtypeStruct((t, d), out_dtype)]
    if emit_sum:
        out_shape = [jax.ShapeDtypeStruct((t, d), F32)] + out_shape
    res = pl.pallas_call(
        functools.partial(_norm_kernel, has_add=add is not None, emit_sum=emit_sum),
        grid=(t // NORM_TM,), in_specs=in_specs, out_specs=[row] * len(out_shape),
        out_shape=out_shape, compiler_params=_params("parallel"), name="rmsnorm")(*ins)
    return res if emit_sum else res[0]


def _matmul_kernel(*refs, has_bias, has_res):
    a_ref, b_ref = refs[:2]
    o_ref = refs[-1]
    acc = jnp.dot(a_ref[...], b_ref[...], preferred_element_type=F32)
    if has_bias:
        acc = acc + refs[2][...]
    if has_res:
        acc = acc + refs[-2][...]
    o_ref[...] = acc.astype(o_ref.dtype)


def matmul(a, b, l, out_dtype, bias=None, res=None, col_blocks=None, name="matmul"):
    m, k = a.shape
    tm, tn = min(MM_TM, m), MM_TN
    first, count = col_blocks if col_blocks is not None else (0, b.shape[2] // tn)
    ins = [a, b]
    in_specs = [pl.BlockSpec((tm, k), lambda i, j: (i, 0)),
                pl.BlockSpec((None, k, tn), lambda i, j: (l, 0, first + j))]
    if bias is not None:
        ins.append(bias)
        in_specs.append(pl.BlockSpec((None, 1, tn), lambda i, j: (l, 0, first + j)))
    if res is not None:
        ins.append(res)
        in_specs.append(pl.BlockSpec((tm, tn), lambda i, j: (i, j)))
    return pl.pallas_call(
        functools.partial(_matmul_kernel, has_bias=bias is not None, has_res=res is not None),
        grid=(m // tm, count), in_specs=in_specs,
        out_specs=pl.BlockSpec((tm, tn), lambda i, j: (i, j)),
        out_shape=jax.ShapeDtypeStruct((m, count * tn), out_dtype),
        compiler_params=_params("parallel", "arbitrary"), name=name)(*ins)


def _shift_rows(x, sh, row):
    return jnp.where(row >= sh, pltpu.roll(x, sh, axis=0), 0.0)


def _decay_kernel(x_ref, col_ref, row_ref):
    x = x_ref[0]
    c = jnp.minimum(x, 0.0) - jnp.log1p(jnp.exp(-jnp.abs(x)))
    s = x.shape[0]
    row = lax.broadcasted_iota(jnp.int32, x.shape, 0)
    sh = 1
    while sh < s:
        c = c + _shift_rows(c, sh, row)
        sh *= 2
    col_ref[0, 0] = c
    for kb in range(s // FOX_TQ):
        row_ref[0, 0, kb] = c[kb * FOX_TQ:(kb + 1) * FOX_TQ, :].T[0:8, :]


def fox_decay(f_logit_rep, b, s):
    nk = s // FOX_TQ
    return pl.pallas_call(
        _decay_kernel, grid=(b, N_HEADS),
        in_specs=[pl.BlockSpec((1, s, LANES), lambda bi, h: (bi, 0, h))],
        out_specs=[pl.BlockSpec((1, 1, s, LANES), lambda bi, h: (bi, h, 0, 0)),
                   pl.BlockSpec((1, 1, nk, 8, FOX_TQ), lambda bi, h: (bi, h, 0, 0, 0))],
        out_shape=[jax.ShapeDtypeStruct((b, N_HEADS, s, LANES), F32),
                   jax.ShapeDtypeStruct((b, N_HEADS, nk, 8, FOX_TQ), F32)],
        compiler_params=_params("parallel", "parallel"), name="fox_decay")(f_logit_rep)


def _fox_kernel(q_ref, k_ref, v_ref, cc_ref, cr_ref, o_ref):
    tq = FOX_TQ
    i = pl.program_id(2)
    q = q_ref[0]
    cum_q = jnp.concatenate([cc_ref[0, 0]] * (tq // LANES), axis=1)
    qpos = lax.broadcasted_iota(jnp.int32, (tq, tq), 0)
    kpos = lax.broadcasted_iota(jnp.int32, (tq, tq), 1)
    scale = HEAD_DIM ** -0.5

    def body(j, carry):
        m, l, acc = carry
        ks = pl.multiple_of(j * tq, tq)
        k = k_ref[0, pl.ds(ks, tq), :]
        v = v_ref[0, pl.ds(ks, tq), :]
        s = lax.dot_general(q, k, (((1,), (1,)), ((), ())), preferred_element_type=F32) * scale
        s = s + (cum_q - cr_ref[0, 0, j][0:1, :])
        s = jnp.where((j < i) | (kpos <= qpos), s, NEG)
        m_new = jnp.maximum(m, jnp.max(s, axis=1, keepdims=True))
        alpha = jnp.exp(m - m_new)
        p = jnp.exp(s - m_new)
        l = alpha * l + jnp.sum(p, axis=1, keepdims=True)
        acc = alpha * acc + jnp.dot(p.astype(BF16), v, preferred_element_type=F32)
        return m_new, l, acc

    init = (jnp.full((tq, 1), NEG, F32), jnp.zeros((tq, 1), F32), jnp.zeros((tq, HEAD_DIM), F32))
    _, l, acc = lax.fori_loop(0, i + 1, body, init)
    o_ref[0] = (acc / l).astype(o_ref.dtype)


def fox_attention(qkv, cum_col, cum_row):
    b, s, _ = qkv.shape
    nk = s // FOX_TQ
    return pl.pallas_call(
        _fox_kernel, grid=(b, N_HEADS, s // FOX_TQ),
        in_specs=[pl.BlockSpec((1, FOX_TQ, HEAD_DIM), lambda bi, h, i: (bi, i, h)),
                  pl.BlockSpec((1, s, HEAD_DIM), lambda bi, h, i: (bi, 0, N_HEADS + h)),
                  pl.BlockSpec((1, s, HEAD_DIM), lambda bi, h, i: (bi, 0, 2 * N_HEADS + h)),
                  pl.BlockSpec((1, 1, FOX_TQ, LANES), lambda bi, h, i: (bi, h, i, 0)),
                  pl.BlockSpec((1, 1, nk, 8, FOX_TQ), lambda bi, h, i: (bi, h, 0, 0, 0))],
        out_specs=pl.BlockSpec((1, FOX_TQ, HEAD_DIM), lambda bi, h, i: (bi, i, h)),
        out_shape=jax.ShapeDtypeStruct((b, s, MIX_WIDTH), BF16),
        compiler_params=_params("parallel", "parallel", "arbitrary"), name="fox_attention")(
            qkv, qkv, qkv, cum_col, cum_row)


_TOEPLITZ_W = 4 * ATT_TQ


def _bias_table_kernel(g_ref, o_ref):
    nkeys = 3 * ATT_TQ
    rows = jnp.broadcast_to(g_ref[0], (ATT_TQ, _TOEPLITZ_W))
    toep = pltpu.roll(rows, 0, 1, stride=1, stride_axis=0)[:, :nkeys]
    qc = lax.broadcasted_iota(jnp.int32, (ATT_TQ, nkeys), 0) // CHUNK
    kc = lax.broadcasted_iota(jnp.int32, (ATT_TQ, nkeys), 1) // CHUNK
    band = kc - qc
    o_ref[0] = jnp.where((band >= 0) & (band <= LEFT_CHUNKS), toep, NEG)


def chunk_bias_table(rel_bias):
    nl, nh, rel = rel_bias.shape
    far = rel_bias[:, :, rel - 1:]
    n_far = 2 * ATT_TQ - REL_CLIP + 1
    n_near = 3 * ATT_TQ - n_far - (rel - 1)
    g = jnp.concatenate([jnp.broadcast_to(far, (nl, nh, n_far)),
                         rel_bias[:, :, rel - 2::-1],
                         jnp.broadcast_to(rel_bias[:, :, :1], (nl, nh, n_near)),
                         jnp.broadcast_to(far, (nl, nh, _TOEPLITZ_W - 3 * ATT_TQ))], axis=2)
    g = g.reshape(nl * nh, 1, _TOEPLITZ_W)
    return pl.pallas_call(
        _bias_table_kernel, grid=(nl * nh,),
        in_specs=[pl.BlockSpec((1, 1, _TOEPLITZ_W), lambda i: (i, 0, 0))],
        out_specs=pl.BlockSpec((1, ATT_TQ, 3 * ATT_TQ), lambda i: (i, 0, 0)),
        out_shape=jax.ShapeDtypeStruct((nl * nh, ATT_TQ, 3 * ATT_TQ), F32),
        compiler_params=_params("parallel"), name="chunk_bias_table")(g)


def _chunk_kernel(q_ref, k_ref, v_ref, bias_ref, o_ref):
    tq = ATT_TQ
    i = pl.program_id(2)
    q = q_ref[0]
    scale = HEAD_DIM ** -0.5
    logits, values = [], []
    for r in range(3):
        kb = i - 2 + r
        ks = pl.multiple_of(jnp.maximum(kb, 0) * tq, tq)
        k = k_ref[0, pl.ds(ks, tq), :]
        values.append(v_ref[0, pl.ds(ks, tq), :])
        s = lax.dot_general(q, k, (((1,), (1,)), ((), ())), preferred_element_type=F32) * scale
        s = s + bias_ref[0, :, r * tq:(r + 1) * tq]
        logits.append(jnp.where(kb >= 0, s, NEG))
    m = jnp.maximum(jnp.maximum(jnp.max(logits[0], axis=1, keepdims=True),
                                jnp.max(logits[1], axis=1, keepdims=True)),
                    jnp.max(logits[2], axis=1, keepdims=True))
    l = jnp.zeros((tq, 1), F32)
    acc = jnp.zeros((tq, HEAD_DIM), F32)
    for r in range(3):
        p = jnp.exp(logits[r] - m)
        l = l + jnp.sum(p, axis=1, keepdims=True)
        acc = acc + jnp.dot(p.astype(BF16), values[r], preferred_element_type=F32)
    o_ref[0] = (acc / l).astype(o_ref.dtype)


def chunk_attention(qkv, bias_tbl, l):
    b, s, _ = qkv.shape
    return pl.pallas_call(
        _chunk_kernel, grid=(b, N_HEADS, s // ATT_TQ),
        in_specs=[pl.BlockSpec((1, ATT_TQ, HEAD_DIM), lambda bi, h, i: (bi, i, h)),
                  pl.BlockSpec((1, s, HEAD_DIM), lambda bi, h, i: (bi, 0, N_HEADS + h)),
                  pl.BlockSpec((1, s, HEAD_DIM), lambda bi, h, i: (bi, 0, 2 * N_HEADS + h)),
                  pl.BlockSpec((1, ATT_TQ, 3 * ATT_TQ), lambda bi, h, i: (l * N_HEADS + h, 0, 0))],
        out_specs=pl.BlockSpec((1, ATT_TQ, HEAD_DIM), lambda bi, h, i: (bi, i, h)),
        out_shape=jax.ShapeDtypeStruct((b, s, MIX_WIDTH), BF16),
        compiler_params=_params("parallel", "parallel", "arbitrary"), name="chunk_attention")(
            qkv, qkv, qkv, bias_tbl)


def _pool_kernel(x_ref, w_ref, scale_ref, o_ref):
    s = x_ref.shape[1]
    row = lax.broadcasted_iota(jnp.int32, (s, LANES), 0)
    for g, win in enumerate(POOL_WINDOWS):
        cols = slice(g * LANES, (g + 1) * LANES)
        x = x_ref[0, :, cols]
        tot, span = x, 1
        while span < win:
            tot = tot + _shift_rows(tot, span, row)
            span *= 2
        count = jnp.minimum(row + 1, win).astype(F32)
        pooled = tot / count - x
        mixed = jnp.dot(pooled.astype(BF16), w_ref[g], preferred_element_type=F32)
        o_ref[0, :, cols] = (mixed * scale_ref[:, cols]).astype(o_ref.dtype)


def _conv_kernel(h_ref, b_ref, c_ref, w_ref, o_ref):
    s = h_ref.shape[1]
    row = lax.broadcasted_iota(jnp.int32, (s, MIX_WIDTH), 0)
    z = c_ref[0] * h_ref[0]
    acc = w_ref[0:1, :] * _shift_rows(z, 2, row)
    acc = acc + w_ref[1:2, :] * _shift_rows(z, 1, row)
    acc = acc + w_ref[2:3, :] * z
    o_ref[0] = (b_ref[0] * acc).astype(o_ref.dtype)


def pool_and_conv(pc, pool_w, pool_scale, conv_w, l):
    b, s, _ = pc.shape
    blk = lambda c: pl.BlockSpec((1, s, MIX_WIDTH), lambda bi: (bi, 0, c))
    out = jax.ShapeDtypeStruct((b, s, MIX_WIDTH), BF16)
    y_pool = pl.pallas_call(
        _pool_kernel, grid=(b,),
        in_specs=[blk(0), pl.BlockSpec((None,) + pool_w.shape[1:], lambda bi: (l, 0, 0, 0)),
                  pl.BlockSpec((None, 1, MIX_WIDTH), lambda bi: (l, 0, 0))],
        out_specs=blk(0), out_shape=out, compiler_params=_params("parallel"), name="pool")(
            pc, pool_w, pool_scale)
    y_conv = pl.pallas_call(
        _conv_kernel, grid=(b,),
        in_specs=[blk(1), blk(2), blk(3), pl.BlockSpec((None, 8, MIX_WIDTH), lambda bi: (l, 0, 0))],
        out_specs=blk(0), out_shape=out, compiler_params=_params("parallel"), name="conv")(
            pc, pc, pc, conv_w)
    return y_pool, y_conv


def _merge_kernel(xn_ref, y0, y1, y2, y3, wg0, wg1, wg2, wg3, bg0, bg1, bg2, bg3, wb_ref, o_ref):
    xn = xn_ref[...]
    acc = None
    for n, (y_ref, wg_ref, bg_ref) in enumerate(((y0, wg0, bg0), (y1, wg1, bg1), (y2, wg2, bg2), (y3, wg3, bg3))):
        gate = jax.nn.sigmoid(jnp.dot(xn, wg_ref[...], preferred_element_type=F32) + bg_ref[...])
        term = gate * jnp.dot(y_ref[...], wb_ref[n], preferred_element_type=F32)
        acc = term if acc is None else acc + term
    o_ref[...] = acc.astype(o_ref.dtype)


def merge_branches(xn, ys, w_rest, b_rest, w_branch, l):
    t, d = xn.shape
    tm, tn = MERGE_TM, MERGE_TN
    nj = D_MODEL // tn
    col = lambda n: (lambda i, j: (l, 0, _G_GATE * (MM_TN // tn) + n * nj + j))
    return pl.pallas_call(
        _merge_kernel, grid=(t // tm, nj),
        in_specs=[pl.BlockSpec((tm, d), lambda i, j: (i, 0))]
        + [pl.BlockSpec((tm, MIX_WIDTH), lambda i, j: (i, 0))] * N_BRANCH
        + [pl.BlockSpec((None, d, tn), col(n)) for n in range(N_BRANCH)]
        + [pl.BlockSpec((None, 1, tn), col(n)) for n in range(N_BRANCH)]
        + [pl.BlockSpec((None, N_BRANCH, MIX_WIDTH, tn), lambda i, j: (l, 0, 0, j))],
        out_specs=pl.BlockSpec((tm, tn), lambda i, j: (i, j)),
        out_shape=jax.ShapeDtypeStruct((t, D_MODEL), BF16),
        compiler_params=_params("parallel", "arbitrary"), name="merge")(
            xn, *ys, *([w_rest] * N_BRANCH), *([b_rest] * N_BRANCH), w_branch)


_CAND_PER_ROW = tuple(PEER_TOPK // (a + 1) for a in range(PEER_TOPK))
_CAND_ROWS = -(-sum(_CAND_PER_ROW) // 8) * 8


def _extract16(x, rank_fill):
    n = x.shape[0]
    rows = lax.broadcasted_iota(jnp.int32, x.shape, 0).astype(F32)
    rank = None if rank_fill is None else jnp.full(x.shape, rank_fill, F32)
    vals = []
    for r in range(PEER_TOPK):
        m = jnp.max(x, axis=0, keepdims=True)
        first = jnp.min(jnp.where(x == m, rows, float(n)), axis=0, keepdims=True)
        hit = rows == first
        vals.append(m)
        x = jnp.where(hit, -jnp.inf, x)
        if rank is not None:
            rank = jnp.where(hit, float(r), rank)
    return jnp.concatenate(vals, axis=0), x, rank


def _peer_select_kernel(q_ref, keys_ref, r2_ref, e2_ref, c1_ref, e1_ref):
    ts = q_ref.shape[0]
    cand_start = np.concatenate([[0], np.cumsum(_CAND_PER_ROW)])
    n_cand = int(cand_start[-1])
    pad = _CAND_ROWS - n_cand
    is_cand = lax.broadcasted_iota(jnp.int32, (_CAND_ROWS, ts), 0) < n_cand
    for h in range(PEER_HEADS):
        halves = []
        for p in range(2):
            c0 = (2 * h + p) * PEER_KEYS
            halves.append(lax.dot_general(keys_ref[h, p], q_ref[:, c0:c0 + PEER_KEYS],
                                          (((1,), (1,)), ((), ())), preferred_element_type=F32))
        s1, s2 = halves
        t1, _, rank1 = _extract16(s1, float(PEER_TOPK))
        t2, _, rank2 = _extract16(s2, float(PEER_TOPK))
        cand = jnp.concatenate([t1[a:a + 1] + t2[0:n] for a, n in enumerate(_CAND_PER_ROW)]
                               + [jnp.full((pad, ts), -jnp.inf, F32)], axis=0)
        _, left, _ = _extract16(cand, None)
        taken = (left == -jnp.inf) & is_cand
        norm = jnp.sum(jnp.where(taken, jnp.exp(cand - cand[0:1]), 0.0), axis=0, keepdims=True)
        taken_f = jnp.where(taken, 1.0, 0.0)
        count1 = jnp.zeros_like(s1)
        for a in range(PEER_TOPK):
            c_a = jnp.sum(taken_f[cand_start[a]:cand_start[a + 1]], axis=0, keepdims=True)
            count1 = jnp.where(rank1 == float(a), c_a, count1)
        r2_ref[h] = rank2.astype(r2_ref.dtype)
        e2_ref[h] = jnp.exp(s2 - t2[0:1]).astype(e2_ref.dtype)
        c1_ref[h] = count1
        e1_ref[h] = jnp.exp(s1 - t1[0:1]) / norm


def peer_select(q, keys, l):
    t = q.shape[0]
    big = pl.BlockSpec((PEER_HEADS, PEER_KEYS, SEL_TS), lambda i: (0, 0, i))
    shape = lambda dt: jax.ShapeDtypeStruct((PEER_HEADS, PEER_KEYS, t), dt)
    return pl.pallas_call(
        _peer_select_kernel, grid=(t // SEL_TS,),
        in_specs=[pl.BlockSpec((SEL_TS, q.shape[1]), lambda i: (i, 0)),
                  pl.BlockSpec((None,) + keys.shape[1:], lambda i: (l, 0, 0, 0, 0))],
        out_specs=[big] * 4, out_shape=[shape(BF16), shape(BF16), shape(F32), shape(F32)],
        compiler_params=_params("parallel"), name="peer_select")(q, keys)


def _gelu_tanh(a):
    c = float(np.sqrt(2.0 / np.pi))
    half = 0.5 * a
    return half + half * jnp.tanh(a * (c + (c * 0.044715) * (a * a)))


def _expert_gate_weights(act, i0, r2_ref, e2_ref, c1_ref, e1_ref):
    tm = act.shape[1]
    parts = []
    for ii in range(act.shape[0] // PEER_KEYS):
        i = i0 + ii
        gate = [None] * (PEER_KEYS // BF16_ROWS)
        for h in range(PEER_HEADS):
            count = jnp.broadcast_to(c1_ref[h, pl.ds(i, 1), :], (BF16_ROWS, tm)).astype(BF16)
            e1 = jnp.broadcast_to(e1_ref[h, pl.ds(i, 1), :], (BF16_ROWS, tm)).astype(BF16)
            for jj in range(len(gate)):
                rows = slice(jj * BF16_ROWS, (jj + 1) * BF16_ROWS)
                term = jnp.where(r2_ref[h, rows, :] < count, e2_ref[h, rows, :] * e1, 0.0)
                gate[jj] = term if gate[jj] is None else gate[jj] + term
        for jj, g in enumerate(gate):
            r0 = ii * PEER_KEYS + jj * BF16_ROWS
            parts.append(act[r0:r0 + BF16_ROWS] * g.astype(F32))
    return jnp.concatenate(parts, axis=0)


def _peer_dense_kernel(x_ref, u_ref, v_ref, r2_ref, e2_ref, c1_ref, e1_ref, o_ref):
    e = pl.program_id(1)

    @pl.when(e == 0)
    def _():
        o_ref[...] = jnp.zeros_like(o_ref)

    act = _gelu_tanh(lax.dot_general(u_ref[...], x_ref[...], (((1,), (1,)), ((), ())),
                                     preferred_element_type=F32))
    wt = _expert_gate_weights(act, e * (PEER_EB // PEER_KEYS), r2_ref, e2_ref, c1_ref, e1_ref)
    o_ref[...] += jnp.dot(wt.T.astype(BF16), v_ref[...], preferred_element_type=F32)


def peer_dense(xn, u, v, sel, l):
    t = xn.shape[0]
    big = pl.BlockSpec((PEER_HEADS, PEER_KEYS, PEER_TM), lambda i, e: (0, 0, i))
    wblk = pl.BlockSpec((None, PEER_EB, D_MODEL), lambda i, e: (l, e, 0))
    return pl.pallas_call(
        _peer_dense_kernel, grid=(t // PEER_TM, PEER_EXPERTS // PEER_EB),
        in_specs=[pl.BlockSpec((PEER_TM, D_MODEL), lambda i, e: (i, 0)), wblk, wblk, big, big, big, big],
        out_specs=pl.BlockSpec((PEER_TM, D_MODEL), lambda i, e: (i, 0)),
        out_shape=jax.ShapeDtypeStruct((t, D_MODEL), F32),
        compiler_params=_params("parallel", "arbitrary"), name="peer_dense")(xn, u, v, *sel)


def kernel(x, norm_mix, w_in, b_in, pool_w, pool_scale, conv_w, rel_bias, w_branch, w_out, norm_ffn,
           peer_wq, peer_keys, peer_u, peer_v, norm_final):
    b, s, d = x.shape
    t = b * s
    depth = w_in.shape[0]
    o_pc = FORGET_OFFSET + N_HEADS

    w_fox, w_rest = w_in[:, :, :FORGET_OFFSET].astype(BF16), w_in[:, :, o_pc:].astype(BF16)
    w_forget = jnp.repeat(w_in[:, :, FORGET_OFFSET:o_pc], LANES, axis=2).astype(BF16)
    b_fox, b_rest = b_in[:, None, :FORGET_OFFSET], b_in[:, None, o_pc:]
    b_forget = jnp.repeat(b_in[:, None, FORGET_OFFSET:o_pc], LANES, axis=2)
    w_branch_b, w_out_b, wq_b = w_branch.astype(BF16), w_out.astype(BF16), peer_wq.astype(BF16)
    u_b, v_b, keys_b, pool_w_b = peer_u.astype(BF16), peer_v.astype(BF16), peer_keys.astype(BF16), pool_w.astype(BF16)
    pool_scale_r = pool_scale.reshape(depth, 1, MIX_WIDTH)
    conv_w_r = jnp.pad(conv_w, ((0, 0), (0, 8 - CONV_K), (0, 0)))
    g_mix, g_ffn = norm_mix.reshape(depth, 1, d), norm_ffn.reshape(depth, 1, d)
    bias_tbl = chunk_bias_table(rel_bias)

    h = x.reshape(t, d)
    y_peer = None
    for l in range(depth):
        if y_peer is None:
            xn = rmsnorm(h, g_mix, l, BF16)
        else:
            h, xn = rmsnorm(h, g_mix, l, BF16, add=y_peer, emit_sum=True)
        proj = lambda g0, g1, dt, name: matmul(xn, w_rest, l, dt, bias=b_rest, col_blocks=(g0, g1 - g0), name=name)
        fox_qkv = matmul(xn, w_fox, l, BF16, bias=b_fox, name="proj_fox").reshape(b, s, -1)
        f_rep = matmul(xn, w_forget, l, F32, bias=b_forget, name="proj_forget").reshape(b, s, -1)
        pc = proj(_G_PC, _G_CHUNK, F32, "proj_pool_conv").reshape(b, s, -1)
        chunk_qkv = proj(_G_CHUNK, _G_GATE, BF16, "proj_chunk").reshape(b, s, -1)

        cum_col, cum_row = fox_decay(f_rep, b, s)
        y_fox = fox_attention(fox_qkv, cum_col, cum_row)
        y_pool, y_conv = pool_and_conv(pc, pool_w_b, pool_scale_r, conv_w_r, l)
        y_chunk = chunk_attention(chunk_qkv, bias_tbl, l)

        ys = [y.reshape(t, MIX_WIDTH) for y in (y_fox, y_pool, y_conv, y_chunk)]
        merged = merge_branches(xn, ys, w_rest, b_rest, w_branch_b, l)
        h = matmul(merged, w_out_b, l, F32, res=h, name="out_proj")

        hn = rmsnorm(h, g_ffn, l, BF16)
        q = matmul(hn, wq_b, l, BF16, name="peer_query")
        sel = peer_select(q, keys_b, l)
        y_peer = peer_dense(hn, u_b, v_b, sel, l)

    out = rmsnorm(h, norm_final.reshape(1, 1, d), 0, F32, add=y_peer)
    return out.reshape(b, s, d)
```

```python
import functools

import numpy as np
import jax
import jax.numpy as jnp
from jax import lax
from jax.experimental import pallas as pl
from jax.experimental.pallas import tpu as pltpu

F32 = jnp.float32
BF16 = jnp.bfloat16

D_MODEL = 2048
HEAD_DIM = 128
MIX_WIDTH = D_MODEL // 4
N_BRANCH = 4
N_HEADS = MIX_WIDTH // HEAD_DIM
POOL_WINDOWS = (2, 4, 8, 16)
CONV_K = 3
CHUNK = 64
LEFT_CHUNKS = 8
REL_CLIP = 256
PEER_HEADS = 8
PEER_KEYS = 128
PEER_EXPERTS = PEER_KEYS * PEER_KEYS
PEER_TOPK = 16
EPS = 1e-6
FORGET_OFFSET = 3 * MIX_WIDTH

LANES = 128
BF16_ROWS = 16
VMEM_LIMIT_BYTES = 56 * 2**20

NEG = -1e30

NORM_TM = 512
MM_TM, MM_TN = 1024, 512
ATT_TQ = 256
FOX_TQ = 512
MERGE_TM, MERGE_TN = 1024, 512
FFN_TM = 512
SEL_TS = 256
PEER_TM, PEER_EB = 512, 1024

_G_PC, _G_CHUNK, _G_GATE = 0, 4, 7


def _params(*sem):
    return pltpu.CompilerParams(dimension_semantics=sem, vmem_limit_bytes=VMEM_LIMIT_BYTES)


def _norm_kernel(*refs, has_add, emit_sum):
    if has_add:
        x_ref, y_ref, g_ref = refs[:3]
        x = x_ref[...] + y_ref[...]
    else:
        x_ref, g_ref = refs[:2]
        x = x_ref[...]
    outs = refs[3 if has_add else 2:]
    if emit_sum:
        outs[0][...] = x
    inv = lax.rsqrt(jnp.mean(x * x, axis=-1, keepdims=True) + EPS)
    outs[-1][...] = ((x * inv) * g_ref[...]).astype(outs[-1].dtype)


def rmsnorm(x, gains, l, out_dtype, add=None, emit_sum=False):
    t, d = x.shape
    row = pl.BlockSpec((NORM_TM, d), lambda i: (i, 0))
    ins = [x] + ([add] if add is not None else []) + [gains]
    in_specs = [row] * (len(ins) - 1) + [pl.BlockSpec((None, 1, d), lambda i: (l, 0, 0))]
    out_shape = [jax.ShapeDtypeStruct((t, d), out_dtype)]
    if emit_sum:
        out_shape = [jax.ShapeDtypeStruct((t, d), F32)] + out_shape
    res = pl.pallas_call(
        functools.partial(_norm_kernel, has_add=add is not None, emit_sum=emit_sum),
        grid=(t // NORM_TM,), in_specs=in_specs, out_specs=[row] * len(out_shape),
        out_shape=out_shape, compiler_params=_params("parallel"), name="rmsnorm")(*ins)
    return res if emit_sum else res[0]


def _matmul_kernel(*refs, has_bias, has_res):
    a_ref, b_ref = refs[:2]
    o_ref = refs[-1]
    acc = jnp.dot(a_ref[...], b_ref[...], preferred_element_type=F32)
    if has_bias:
        acc = acc + refs[2][...]
    if has_res:
        acc = acc + refs[-2][...]
    o_ref[...] = acc.astype(o_ref.dtype)


def matmul(a, b, l, out_dtype, bias=None, res=None, col_blocks=None, name="matmul"):
    m, k = a.shape
    tm, tn = min(MM_TM, m), MM_TN
    first, count = col_blocks if col_blocks is not None else (0, b.shape[2] // tn)
    ins = [a, b]
    in_specs = [pl.BlockSpec((tm, k), lambda i, j: (i, 0)),
                pl.BlockSpec((None, k, tn), lambda i, j: (l, 0, first + j))]
    if bias is not None:
        ins.append(bias)
        in_specs.append(pl.BlockSpec((None, 1, tn), lambda i, j: (l, 0, first + j)))
    if res is not None:
        ins.append(res)
        in_specs.append(pl.BlockSpec((tm, tn), lambda i, j: (i, j)))
    return pl.pallas_call(
        functools.partial(_matmul_kernel, has_bias=bias is not None, has_res=res is not None),
        grid=(m // tm, count), in_specs=in_specs,
        out_specs=pl.BlockSpec((tm, tn), lambda i, j: (i, j)),
        out_shape=jax.ShapeDtypeStruct((m, count * tn), out_dtype),
        compiler_params=_params("parallel", "arbitrary"), name=name)(*ins)


def _rebase_kernel(a_ref, b_ref, o_ref, *, shift):
    tn = o_ref.shape[1]
    wide = jnp.concatenate([a_ref[...], b_ref[...]], axis=1)
    o_ref[...] = wide[:, shift:shift + tn].astype(o_ref.dtype)


def rebase_columns(w, start, width, out_dtype):
    nl, k, n = w.shape
    tn = MM_TN
    base, shift = start // LANES * LANES, start % LANES
    assert base % tn == 0 and width % tn == 0 and start + width <= n
    return pl.pallas_call(
        functools.partial(_rebase_kernel, shift=shift), grid=(nl, width // tn),
        in_specs=[pl.BlockSpec((None, k, tn), lambda l, j: (l, 0, base // tn + j)),
                  pl.BlockSpec((None, k, LANES), lambda l, j: (l, 0, (base + tn) // LANES + j * (tn // LANES)))],
        out_specs=pl.BlockSpec((None, k, tn), lambda l, j: (l, 0, j)),
        out_shape=jax.ShapeDtypeStruct((nl, k, width), out_dtype),
        compiler_params=_params("parallel", "parallel"), name="rebase_columns")(w, w)


def _shift_rows(x, sh, row):
    return jnp.where(row >= sh, pltpu.roll(x, sh, axis=0), 0.0)


def _decay_kernel(x_ref, col_ref, row_ref):
    x = x_ref[0]
    c = jnp.minimum(x, 0.0) - jnp.log1p(jnp.exp(-jnp.abs(x)))
    s = x.shape[0]
    row = lax.broadcasted_iota(jnp.int32, x.shape, 0)
    sh = 1
    while sh < s:
        c = c + _shift_rows(c, sh, row)
        sh *= 2
    col_ref[0, 0] = c
    for kb in range(s // FOX_TQ):
        row_ref[0, 0, kb] = c[kb * FOX_TQ:(kb + 1) * FOX_TQ, :].T[0:8, :]


def fox_decay(f_logit_rep, b, s):
    nk = s // FOX_TQ
    return pl.pallas_call(
        _decay_kernel, grid=(b, N_HEADS),
        in_specs=[pl.BlockSpec((1, s, LANES), lambda bi, h: (bi, 0, h))],
        out_specs=[pl.BlockSpec((1, 1, s, LANES), lambda bi, h: (bi, h, 0, 0)),
                   pl.BlockSpec((1, 1, nk, 8, FOX_TQ), lambda bi, h: (bi, h, 0, 0, 0))],
        out_shape=[jax.ShapeDtypeStruct((b, N_HEADS, s, LANES), F32),
                   jax.ShapeDtypeStruct((b, N_HEADS, nk, 8, FOX_TQ), F32)],
        compiler_params=_params("parallel", "parallel"), name="fox_decay")(f_logit_rep)


def _fox_kernel(q_ref, k_ref, v_ref, cc_ref, cr_ref, o_ref):
    tq = FOX_TQ
    i = pl.program_id(2)
    q = q_ref[0]
    cum_q = jnp.concatenate([cc_ref[0, 0]] * (tq // LANES), axis=1)
    qpos = lax.broadcasted_iota(jnp.int32, (tq, tq), 0)
    kpos = lax.broadcasted_iota(jnp.int32, (tq, tq), 1)
    scale = HEAD_DIM ** -0.5

    def body(j, carry):
        m, l, acc = carry
        ks = pl.multiple_of(j * tq, tq)
        k = k_ref[0, pl.ds(ks, tq), :]
        v = v_ref[0, pl.ds(ks, tq), :]
        s = lax.dot_general(q, k, (((1,), (1,)), ((), ())), preferred_element_type=F32) * scale
        s = s + (cum_q - cr_ref[0, 0, j][0:1, :])
        s = jnp.where((j < i) | (kpos <= qpos), s, NEG)
        m_new = jnp.maximum(m, jnp.max(s, axis=1, keepdims=True))
        alpha = jnp.exp(m - m_new)
        p = jnp.exp(s - m_new)
        l = alpha * l + jnp.sum(p, axis=1, keepdims=True)
        acc = alpha * acc + jnp.dot(p.astype(BF16), v, preferred_element_type=F32)
        return m_new, l, acc

    init = (jnp.full((tq, 1), NEG, F32), jnp.zeros((tq, 1), F32), jnp.zeros((tq, HEAD_DIM), F32))
    _, l, acc = lax.fori_loop(0, i + 1, body, init)
    o_ref[0] = (acc / l).astype(o_ref.dtype)


def fox_attention(qkv, cum_col, cum_row):
    b, s, _ = qkv.shape
    nk = s // FOX_TQ
    return pl.pallas_call(
        _fox_kernel, grid=(b, N_HEADS, s // FOX_TQ),
        in_specs=[pl.BlockSpec((1, FOX_TQ, HEAD_DIM), lambda bi, h, i: (bi, i, h)),
                  pl.BlockSpec((1, s, HEAD_DIM), lambda bi, h, i: (bi, 0, N_HEADS + h)),
                  pl.BlockSpec((1, s, HEAD_DIM), lambda bi, h, i: (bi, 0, 2 * N_HEADS + h)),
                  pl.BlockSpec((1, 1, FOX_TQ, LANES), lambda bi, h, i: (bi, h, i, 0)),
                  pl.BlockSpec((1, 1, nk, 8, FOX_TQ), lambda bi, h, i: (bi, h, 0, 0, 0))],
        out_specs=pl.BlockSpec((1, FOX_TQ, HEAD_DIM), lambda bi, h, i: (bi, i, h)),
        out_shape=jax.ShapeDtypeStruct((b, s, MIX_WIDTH), BF16),
        compiler_params=_params("parallel", "parallel", "arbitrary"), name="fox_attention")(
            qkv, qkv, qkv, cum_col, cum_row)


_TOEPLITZ_W = 4 * ATT_TQ


def _bias_table_kernel(g_ref, o_ref):
    nkeys = 3 * ATT_TQ
    rows = jnp.broadcast_to(g_ref[0], (ATT_TQ, _TOEPLITZ_W))
    toep = pltpu.roll(rows, 0, 1, stride=1, stride_axis=0)[:, :nkeys]
    qc = lax.broadcasted_iota(jnp.int32, (ATT_TQ, nkeys), 0) // CHUNK
    kc = lax.broadcasted_iota(jnp.int32, (ATT_TQ, nkeys), 1) // CHUNK
    band = kc - qc
    o_ref[0] = jnp.where((band >= 0) & (band <= LEFT_CHUNKS), toep, NEG)


def chunk_bias_table(rel_bias):
    nl, nh, rel = rel_bias.shape
    far = rel_bias[:, :, rel - 1:]
    n_far = 2 * ATT_TQ - REL_CLIP + 1
    n_near = 3 * ATT_TQ - n_far - (rel - 1)
    g = jnp.concatenate([jnp.broadcast_to(far, (nl, nh, n_far)),
                         rel_bias[:, :, rel - 2::-1],
                         jnp.broadcast_to(rel_bias[:, :, :1], (nl, nh, n_near)),
                         jnp.broadcast_to(far, (nl, nh, _TOEPLITZ_W - 3 * ATT_TQ))], axis=2)
    g = g.reshape(nl * nh, 1, _TOEPLITZ_W)
    return pl.pallas_call(
        _bias_table_kernel, grid=(nl * nh,),
        in_specs=[pl.BlockSpec((1, 1, _TOEPLITZ_W), lambda i: (i, 0, 0))],
        out_specs=pl.BlockSpec((1, ATT_TQ, 3 * ATT_TQ), lambda i: (i, 0, 0)),
        out_shape=jax.ShapeDtypeStruct((nl * nh, ATT_TQ, 3 * ATT_TQ), F32),
        compiler_params=_params("parallel"), name="chunk_bias_table")(g)


def _chunk_kernel(q_ref, k_ref, v_ref, bias_ref, o_ref):
    tq = ATT_TQ
    i = pl.program_id(1)
    scale = HEAD_DIM ** -0.5
    for h in range(N_HEADS):
        cols = slice(h * HEAD_DIM, (h + 1) * HEAD_DIM)
        q = q_ref[0, :, cols]
        logits, values = [], []
        for r in range(3):
            kb = i - 2 + r
            ks = pl.multiple_of(jnp.maximum(kb, 0) * tq, tq)
            k = k_ref[0, pl.ds(ks, tq), cols]
            values.append(v_ref[0, pl.ds(ks, tq), cols])
            s = lax.dot_general(q, k, (((1,), (1,)), ((), ())), preferred_element_type=F32) * scale
            s = s + bias_ref[h, :, r * tq:(r + 1) * tq]
            logits.append(jnp.where(kb >= 0, s, NEG))
        m = jnp.maximum(jnp.maximum(jnp.max(logits[0], axis=1, keepdims=True),
                                    jnp.max(logits[1], axis=1, keepdims=True)),
                        jnp.max(logits[2], axis=1, keepdims=True))
        l = jnp.zeros((tq, 1), F32)
        acc = jnp.zeros((tq, HEAD_DIM), F32)
        for r in range(3):
            p = jnp.exp(logits[r] - m)
            l = l + jnp.sum(p, axis=1, keepdims=True)
            acc = acc + jnp.dot(p.astype(BF16), values[r], preferred_element_type=F32)
        o_ref[0, :, cols] = (acc / l).astype(o_ref.dtype)


def chunk_attention(qkv, bias_tbl, l):
    b, s, _ = qkv.shape
    return pl.pallas_call(
        _chunk_kernel, grid=(b, s // ATT_TQ),
        in_specs=[pl.BlockSpec((1, ATT_TQ, MIX_WIDTH), lambda bi, i: (bi, i, 0)),
                  pl.BlockSpec((1, s, MIX_WIDTH), lambda bi, i: (bi, 0, 1)),
                  pl.BlockSpec((1, s, MIX_WIDTH), lambda bi, i: (bi, 0, 2)),
                  pl.BlockSpec((N_HEADS, ATT_TQ, 3 * ATT_TQ), lambda bi, i: (l, 0, 0))],
        out_specs=pl.BlockSpec((1, ATT_TQ, MIX_WIDTH), lambda bi, i: (bi, i, 0)),
        out_shape=jax.ShapeDtypeStruct((b, s, MIX_WIDTH), BF16),
        compiler_params=_params("parallel", "arbitrary"), name="chunk_attention")(
            qkv, qkv, qkv, bias_tbl)


def _pool_kernel(x_ref, w_ref, scale_ref, o_ref):
    s = x_ref.shape[1]
    row = lax.broadcasted_iota(jnp.int32, (s, LANES), 0)
    for g, win in enumerate(POOL_WINDOWS):
        cols = slice(g * LANES, (g + 1) * LANES)
        x = x_ref[0, :, cols]
        tot, span = x, 1
        while span < win:
            tot = tot + _shift_rows(tot, span, row)
            span *= 2
        count = jnp.minimum(row + 1, win).astype(F32)
        pooled = tot / count - x
        mixed = jnp.dot(pooled.astype(BF16), w_ref[g], preferred_element_type=F32)
        o_ref[0, :, cols] = (mixed * scale_ref[:, cols]).astype(o_ref.dtype)


def _conv_kernel(h_ref, b_ref, c_ref, w_ref, o_ref):
    s = h_ref.shape[1]
    row = lax.broadcasted_iota(jnp.int32, (s, MIX_WIDTH), 0)
    z = c_ref[0] * h_ref[0]
    acc = w_ref[0:1, :] * _shift_rows(z, 2, row)
    acc = acc + w_ref[1:2, :] * _shift_rows(z, 1, row)
    acc = acc + w_ref[2:3, :] * z
    o_ref[0] = (b_ref[0] * acc).astype(o_ref.dtype)


def pool_and_conv(pc, pool_w, pool_scale, conv_w, l):
    b, s, _ = pc.shape
    blk = lambda c: pl.BlockSpec((1, s, MIX_WIDTH), lambda bi: (bi, 0, c))
    out = jax.ShapeDtypeStruct((b, s, MIX_WIDTH), BF16)
    y_pool = pl.pallas_call(
        _pool_kernel, grid=(b,),
        in_specs=[blk(0), pl.BlockSpec((None,) + pool_w.shape[1:], lambda bi: (l, 0, 0, 0)),
                  pl.BlockSpec((None, 1, MIX_WIDTH), lambda bi: (l, 0, 0))],
        out_specs=blk(0), out_shape=out, compiler_params=_params("parallel"), name="pool")(
            pc, pool_w, pool_scale)
    y_conv = pl.pallas_call(
        _conv_kernel, grid=(b,),
        in_specs=[blk(1), blk(2), blk(3), pl.BlockSpec((None, 8, MIX_WIDTH), lambda bi: (l, 0, 0))],
        out_specs=blk(0), out_shape=out, compiler_params=_params("parallel"), name="conv")(
            pc, pc, pc, conv_w)
    return y_pool, y_conv


def _merge_kernel(xn_ref, y0, y1, y2, y3, wg0, wg1, wg2, wg3, bg0, bg1, bg2, bg3, wb_ref, o_ref):
    xn = xn_ref[...]
    acc = None
    for n, (y_ref, wg_ref, bg_ref) in enumerate(((y0, wg0, bg0), (y1, wg1, bg1), (y2, wg2, bg2), (y3, wg3, bg3))):
        gate = jax.nn.sigmoid(jnp.dot(xn, wg_ref[...], preferred_element_type=F32) + bg_ref[...])
        term = gate * jnp.dot(y_ref[...], wb_ref[n], preferred_element_type=F32)
        acc = term if acc is None else acc + term
    o_ref[...] = acc.astype(o_ref.dtype)


def merge_branches(xn, ys, w_rest, b_rest, w_branch, l):
    t, d = xn.shape
    tm, tn = MERGE_TM, MERGE_TN
    nj = D_MODEL // tn
    col = lambda n: (lambda i, j: (l, 0, _G_GATE * (MM_TN // tn) + n * nj + j))
    return pl.pallas_call(
        _merge_kernel, grid=(t // tm, nj),
        in_specs=[pl.BlockSpec((tm, d), lambda i, j: (i, 0))]
        + [pl.BlockSpec((tm, MIX_WIDTH), lambda i, j: (i, 0))] * N_BRANCH
        + [pl.BlockSpec((None, d, tn), col(n)) for n in range(N_BRANCH)]
        + [pl.BlockSpec((None, 1, tn), col(n)) for n in range(N_BRANCH)]
        + [pl.BlockSpec((None, N_BRANCH, MIX_WIDTH, tn), lambda i, j: (l, 0, 0, j))],
        out_specs=pl.BlockSpec((tm, tn), lambda i, j: (i, j)),
        out_shape=jax.ShapeDtypeStruct((t, D_MODEL), BF16),
        compiler_params=_params("parallel", "arbitrary"), name="merge")(
            xn, *ys, *([w_rest] * N_BRANCH), *([b_rest] * N_BRANCH), w_branch)


def _ffn_pre_kernel(m_ref, wo_ref, h_ref, g_ref, wq_ref, hout_ref, hn_ref, q_ref):
    h = h_ref[...] + jnp.dot(m_ref[...], wo_ref[...], preferred_element_type=F32)
    hout_ref[...] = h
    inv = lax.rsqrt(jnp.mean(h * h, axis=-1, keepdims=True) + EPS)
    hn = ((h * inv) * g_ref[...]).astype(BF16)
    hn_ref[...] = hn
    q_ref[...] = jnp.dot(hn, wq_ref[...], preferred_element_type=F32).astype(q_ref.dtype)


def ffn_pre(merged, w_out, h, gains, wq, l):
    t, d = h.shape
    tm = FFN_TM
    row = lambda: pl.BlockSpec((tm, d), lambda i: (i, 0))
    whole = lambda w: pl.BlockSpec((None,) + w.shape[1:], lambda i: (l, 0, 0), pipeline_mode=pl.Buffered(1))
    return pl.pallas_call(
        _ffn_pre_kernel, grid=(t // tm,),
        in_specs=[row(), whole(w_out), row(), pl.BlockSpec((None, 1, d), lambda i: (l, 0, 0)), whole(wq)],
        out_specs=[row(), row(), row()],
        out_shape=[jax.ShapeDtypeStruct((t, d), F32), jax.ShapeDtypeStruct((t, d), BF16),
                   jax.ShapeDtypeStruct((t, wq.shape[2]), BF16)],
        compiler_params=_params("parallel"), name="ffn_pre")(merged, w_out, h, gains, wq)


_CAND_PER_ROW = tuple(PEER_TOPK // (a + 1) for a in range(PEER_TOPK))
_CAND_ROWS = -(-sum(_CAND_PER_ROW) // 8) * 8
_TAKEN = 2.0 ** 100


def _extract16(x, rank_fill, stable):
    n = x.shape[0]
    if not stable and rank_fill is not None:
        vals = []
        for r in range(PEER_TOPK):
            m = jnp.max(x, axis=0, keepdims=True)
            vals.append(m)
            x = jnp.where(x == m, -_TAKEN * (1.0 + r / 32.0), x)
        taken = x <= -0.5 * _TAKEN
        rank = jnp.where(taken, jnp.floor(x * (-32.0 / _TAKEN) + 0.5) - 32.0, rank_fill)
        return jnp.concatenate(vals, axis=0), jnp.where(taken, -jnp.inf, x), rank
    rows = lax.broadcasted_iota(jnp.int32, x.shape, 0).astype(F32)
    rank = None if rank_fill is None else jnp.full(x.shape, rank_fill, F32)
    vals = []
    for r in range(PEER_TOPK):
        m = jnp.max(x, axis=0, keepdims=True)
        hit = x == m
        if stable:
            first = jnp.min(jnp.where(hit, rows, float(n)), axis=0, keepdims=True)
            hit = rows == first
        vals.append(m)
        x = jnp.where(hit, -jnp.inf, x)
        if rank is not None:
            rank = jnp.where(hit, float(r), rank)
    return jnp.concatenate(vals, axis=0), x, rank


def _select_heads(q_ref, keys_ref, r2_ref, e2_ref, c1_ref, e1_ref, stable):
    ts = q_ref.shape[0]
    cand_start = np.concatenate([[0], np.cumsum(_CAND_PER_ROW)])
    n_cand = int(cand_start[-1])
    pad = _CAND_ROWS - n_cand
    is_cand = lax.broadcasted_iota(jnp.int32, (_CAND_ROWS, ts), 0) < n_cand
    n_taken = lambda left: jnp.sum(jnp.where(left == -jnp.inf, 1.0, 0.0), axis=0, keepdims=True)
    tied = jnp.zeros((1, ts), F32)
    for h in range(PEER_HEADS):
        halves = []
        for p in range(2):
            c0 = (2 * h + p) * PEER_KEYS
            halves.append(lax.dot_general(keys_ref[h, p], q_ref[:, c0:c0 + PEER_KEYS],
                                          (((1,), (1,)), ((), ())), preferred_element_type=F32))
        s1, s2 = halves
        t1, left1, rank1 = _extract16(s1, float(PEER_TOPK), stable)
        t2, left2, rank2 = _extract16(s2, float(PEER_TOPK), stable)
        cand = jnp.concatenate([t1[a:a + 1] + t2[0:n] for a, n in enumerate(_CAND_PER_ROW)]
                               + [jnp.full((pad, ts), -jnp.inf, F32)], axis=0)
        _, left, _ = _extract16(cand, None, stable)
        taken = (left == -jnp.inf) & is_cand
        if not stable:
            wrong = (n_taken(left1) != PEER_TOPK) | (n_taken(left2) != PEER_TOPK) | (n_taken(left) != PEER_TOPK + pad)
            tied = jnp.where(wrong, 1.0, tied)
        norm = jnp.sum(jnp.where(taken, jnp.exp(cand - cand[0:1]), 0.0), axis=0, keepdims=True)
        taken_f = jnp.where(taken, 1.0, 0.0)
        count1 = jnp.zeros_like(s1)
        for a in range(PEER_TOPK):
            c_a = jnp.sum(taken_f[cand_start[a]:cand_start[a + 1]], axis=0, keepdims=True)
            count1 = jnp.where(rank1 == float(a), c_a, count1)
        r2_ref[h] = rank2.astype(r2_ref.dtype)
        e2_ref[h] = jnp.exp(s2 - t2[0:1]).astype(e2_ref.dtype)
        c1_ref[h] = count1
        e1_ref[h] = jnp.exp(s1 - t1[0:1]) / norm
    return tied


def _peer_select_kernel(q_ref, keys_ref, r2_ref, e2_ref, c1_ref, e1_ref):
    outs = (r2_ref, e2_ref, c1_ref, e1_ref)
    tied = _select_heads(q_ref, keys_ref, *outs, stable=False)

    @pl.when(jnp.max(tied) > 0.0)
    def _():
        _select_heads(q_ref, keys_ref, *outs, stable=True)


def peer_select(q, keys, l):
    t = q.shape[0]
    big = pl.BlockSpec((PEER_HEADS, PEER_KEYS, SEL_TS), lambda i: (0, 0, i))
    shape = lambda dt: jax.ShapeDtypeStruct((PEER_HEADS, PEER_KEYS, t), dt)
    return pl.pallas_call(
        _peer_select_kernel, grid=(t // SEL_TS,),
        in_specs=[pl.BlockSpec((SEL_TS, q.shape[1]), lambda i: (i, 0)),
                  pl.BlockSpec((None,) + keys.shape[1:], lambda i: (l, 0, 0, 0, 0))],
        out_specs=[big] * 4, out_shape=[shape(BF16), shape(BF16), shape(F32), shape(F32)],
        compiler_params=_params("parallel"), name="peer_select")(q, keys)


def _gelu_tanh(a):
    c = float(np.sqrt(2.0 / np.pi))
    half = 0.5 * a
    return half + half * jnp.tanh(a * (c + (c * 0.044715) * (a * a)))


def _expert_gate_weights(act, i0, r2_ref, e2_ref, c1_ref, e1_ref):
    tm = act.shape[1]
    parts = []
    for ii in range(act.shape[0] // PEER_KEYS):
        i = i0 + ii
        gate = [None] * (PEER_KEYS // BF16_ROWS)
        for h in range(PEER_HEADS):
            count = jnp.broadcast_to(c1_ref[h, pl.ds(i, 1), :], (BF16_ROWS, tm)).astype(BF16)
            e1 = jnp.broadcast_to(e1_ref[h, pl.ds(i, 1), :], (BF16_ROWS, tm)).astype(BF16)
            for jj in range(len(gate)):
                rows = slice(jj * BF16_ROWS, (jj + 1) * BF16_ROWS)
                term = jnp.where(r2_ref[h, rows, :] < count, e2_ref[h, rows, :] * e1, 0.0)
                gate[jj] = term if gate[jj] is None else gate[jj] + term
        for jj, g in enumerate(gate):
            r0 = ii * PEER_KEYS + jj * BF16_ROWS
            parts.append(act[r0:r0 + BF16_ROWS] * g)
    return jnp.concatenate(parts, axis=0)


def _peer_dense_kernel(x_ref, u_ref, v_ref, r2_ref, e2_ref, c1_ref, e1_ref, o_ref):
    e = pl.program_id(1)

    @pl.when(e == 0)
    def _():
        o_ref[...] = jnp.zeros_like(o_ref)

    act = _gelu_tanh(lax.dot_general(u_ref[...], x_ref[...], (((1,), (1,)), ((), ())),
                                     preferred_element_type=F32)).astype(BF16)
    wt = _expert_gate_weights(act, e * (PEER_EB // PEER_KEYS), r2_ref, e2_ref, c1_ref, e1_ref)
    o_ref[...] += jnp.dot(wt.T, v_ref[...], preferred_element_type=F32)


def peer_dense(xn, u, v, sel, l):
    t = xn.shape[0]
    big = pl.BlockSpec((PEER_HEADS, PEER_KEYS, PEER_TM), lambda i, e: (0, 0, i))
    wblk = pl.BlockSpec((None, PEER_EB, D_MODEL), lambda i, e: (l, e, 0))
    return pl.pallas_call(
        _peer_dense_kernel, grid=(t // PEER_TM, PEER_EXPERTS // PEER_EB),
        in_specs=[pl.BlockSpec((PEER_TM, D_MODEL), lambda i, e: (i, 0)), wblk, wblk, big, big, big, big],
        out_specs=pl.BlockSpec((PEER_TM, D_MODEL), lambda i, e: (i, 0)),
        out_shape=jax.ShapeDtypeStruct((t, D_MODEL), F32),
        compiler_params=_params("parallel", "arbitrary"), name="peer_dense")(xn, u, v, *sel)


def kernel(x, norm_mix, w_in, b_in, pool_w, pool_scale, conv_w, rel_bias, w_branch, w_out, norm_ffn,
           peer_wq, peer_keys, peer_u, peer_v, norm_final):
    b, s, d = x.shape
    t = b * s
    depth = w_in.shape[0]
    o_pc = FORGET_OFFSET + N_HEADS

    w_fox = w_in[:, :, :FORGET_OFFSET].astype(BF16)
    w_rest = rebase_columns(w_in, o_pc, w_in.shape[2] - o_pc, BF16)
    w_forget = jnp.repeat(w_in[:, :, FORGET_OFFSET:o_pc], LANES, axis=2).astype(BF16)
    b_fox, b_rest = b_in[:, None, :FORGET_OFFSET], b_in[:, None, o_pc:]
    b_forget = jnp.repeat(b_in[:, None, FORGET_OFFSET:o_pc], LANES, axis=2)
    w_branch_b, w_out_b, wq_b = w_branch.astype(BF16), w_out.astype(BF16), peer_wq.astype(BF16)
    u_b, v_b, keys_b, pool_w_b = peer_u.astype(BF16), peer_v.astype(BF16), peer_keys.astype(BF16), pool_w.astype(BF16)
    pool_scale_r = pool_scale.reshape(depth, 1, MIX_WIDTH)
    conv_w_r = jnp.pad(conv_w, ((0, 0), (0, 8 - CONV_K), (0, 0)))
    g_mix, g_ffn = norm_mix.reshape(depth, 1, d), norm_ffn.reshape(depth, 1, d)
    bias_tbl = chunk_bias_table(rel_bias)

    h = x.reshape(t, d)
    y_peer = None
    for l in range(depth):
        if y_peer is None:
            xn = rmsnorm(h, g_mix, l, BF16)
        else:
            h, xn = rmsnorm(h, g_mix, l, BF16, add=y_peer, emit_sum=True)
        proj = lambda g0, g1, dt, name: matmul(xn, w_rest, l, dt, bias=b_rest, col_blocks=(g0, g1 - g0), name=name)
        fox_qkv = matmul(xn, w_fox, l, BF16, bias=b_fox, name="proj_fox").reshape(b, s, -1)
        f_rep = matmul(xn, w_forget, l, F32, bias=b_forget, name="proj_forget").reshape(b, s, -1)
        pc = proj(_G_PC, _G_CHUNK, F32, "proj_pool_conv").reshape(b, s, -1)
        chunk_qkv = proj(_G_CHUNK, _G_GATE, BF16, "proj_chunk").reshape(b, s, -1)

        cum_col, cum_row = fox_decay(f_rep, b, s)
        y_fox = fox_attention(fox_qkv, cum_col, cum_row)
        y_pool, y_conv = pool_and_conv(pc, pool_w_b, pool_scale_r, conv_w_r, l)
        y_chunk = chunk_attention(chunk_qkv, bias_tbl, l)

        ys = [y.reshape(t, MIX_WIDTH) for y in (y_fox, y_pool, y_conv, y_chunk)]
        merged = merge_branches(xn, ys, w_rest, b_rest, w_branch_b, l)
        h, hn, q = ffn_pre(merged, w_out_b, h, g_ffn, wq_b, l)
        sel = peer_select(q, keys_b, l)
        y_peer = peer_dense(hn, u_b, v_b, sel, l)

    out = rmsnorm(h, norm_final.reshape(1, 1, d), 0, F32, add=y_peer)
    return out.reshape(b, s, d)
```
